```python
import jax, jax.numpy as jnp
from jax import lax
import numpy as np

D_MODEL = 1024
BATCH = 4
SEQ = 4096
DEPTH = 4

CTX_LEN = 256
GRID_W = 64
N_MIXERS = 3
EPS = 1e-6

NA_HEADS = 16
NA_HEAD_DIM = D_MODEL // NA_HEADS
NA_KH = 8
NA_KW = 16
NA_SCALE = NA_HEAD_DIM ** -0.5

RET_HEADS = 4
RET_QK_DIM = D_MODEL // RET_HEADS
RET_V_DIM = 2 * D_MODEL // RET_HEADS
RET_CHUNK = 128
ROPE_BASE = 10000.0

CONV_WIDTH = 31

FFN_HIDDEN = 2816
FFN_CONV_WIDTH = 3

N_NA_LAYERS = (DEPTH + 2) // 3
N_RET_LAYERS = (DEPTH + 1) // 3
N_CONV_LAYERS = DEPTH // 3

kernel_name = 'hybrid_na_retnet_conformer_dit'

F32 = jnp.float32


def rms_norm(x, gain):
    x32 = x.astype(F32)
    y = x32 * lax.rsqrt(jnp.mean(x32 * x32, axis=-1, keepdims=True) + EPS)
    return (y * gain.astype(F32)).astype(x.dtype)


def modulate_norm(x, gain, shift, scale):
    return rms_norm(x, gain) * (1 + scale) + shift


def layer_norm(x, gain, bias):
    x32 = x.astype(F32)
    mu = jnp.mean(x32, axis=-1, keepdims=True)
    xc = x32 - mu
    var = jnp.mean(xc * xc, axis=-1, keepdims=True)
    return (xc * lax.rsqrt(var + EPS) * gain.astype(F32) + bias.astype(F32)).astype(x.dtype)


def split_heads(x, n_heads):
    b, n, _ = x.shape
    return x.reshape(b, n, n_heads, -1).transpose(0, 2, 1, 3)


def merge_heads(x):
    b, h, n, d = x.shape
    return x.transpose(0, 2, 1, 3).reshape(b, n, h * d)


def depthwise_conv(x, w, b):
    y = lax.conv_general_dilated(x, w[:, None, :], window_strides=(1,), padding='SAME',
                                 dimension_numbers=('NWC', 'WIO', 'NWC'),
                                 feature_group_count=x.shape[-1])
    return y + b


def axial_rope_tables(n, dim):
    t = jnp.arange(n)
    row = (t // GRID_W).astype(F32)
    col = (t % GRID_W).astype(F32)
    n_freq = dim // 4
    inv_freq = jnp.power(ROPE_BASE, -jnp.arange(n_freq, dtype=F32) / n_freq)
    ang = jnp.concatenate([row[:, None] * inv_freq, col[:, None] * inv_freq], axis=-1)
    return jnp.cos(ang), jnp.sin(ang)


def apply_rope(x, cos, sin):
    half = x.shape[-1] // 2
    x1 = x[..., :half].astype(F32)
    x2 = x[..., half:].astype(F32)
    return jnp.concatenate([x1 * cos - x2 * sin, x1 * sin + x2 * cos], axis=-1).astype(x.dtype)


def head_rms_norm(x, gain):
    x32 = x.astype(F32)
    y = x32 * lax.rsqrt(jnp.mean(x32 * x32, axis=-1, keepdims=True) + EPS)
    return (y * gain.astype(F32)).astype(x.dtype)


def na_mixer(hx, hc, w_qkv, w_o, q_gain, k_gain, rpb, need_ctx):
    b, n, _ = hx.shape
    rows = n // GRID_W
    kh = min(NA_KH, rows)
    nb = kh * NA_KW

    def qkv(h):
        q, k, v = jnp.split(h @ w_qkv, 3, axis=-1)
        q = head_rms_norm(split_heads(q, NA_HEADS), q_gain)
        k = head_rms_norm(split_heads(k, NA_HEADS), k_gain)
        return q, k, split_heads(v, NA_HEADS)

    qx, kx, vx = qkv(hx)
    qc, kc, vc = qkv(hc)

    r = jnp.arange(rows)
    col = jnp.arange(GRID_W)
    row_start = jnp.clip(r - kh // 2, 0, rows - kh)
    col_start = jnp.clip(col - NA_KW // 2, 0, GRID_W - NA_KW)
    key_r = row_start[:, None] + jnp.arange(kh)
    key_c = col_start[:, None] + jnp.arange(NA_KW)
    nb_idx = (key_r[:, None, :, None] * GRID_W + key_c[None, :, None, :]).reshape(rows, GRID_W, nb)
    rel_r = key_r - r[:, None] + (NA_KH - 1)
    rel_c = key_c - col[:, None] + (NA_KW - 1)
    bias = rpb[:, rel_r[:, None, :, None], rel_c[None, :, None, :]]
    bias = jnp.moveaxis(bias.reshape(NA_HEADS, rows, GRID_W, nb), 0, 1)
    q_rows = jnp.moveaxis(qx.reshape(b, NA_HEADS, rows, GRID_W, NA_HEAD_DIM), 2, 0)

    def row_block(args):
        q_r, idx_r, bias_r = args
        k_nb = jnp.take(kx, idx_r, axis=2)
        v_nb = jnp.take(vx, idx_r, axis=2)
        s_nb = jnp.einsum('bhqd,bhqkd->bhqk', q_r, k_nb) * NA_SCALE + bias_r
        s_ctx = jnp.einsum('bhqd,bhkd->bhqk', q_r, kc) * NA_SCALE
        p = jax.nn.softmax(jnp.concatenate([s_nb, s_ctx], axis=-1).astype(F32), axis=-1).astype(v_nb.dtype)
        return (jnp.einsum('bhqk,bhqkd->bhqd', p[..., :nb], v_nb)
                + jnp.einsum('bhqk,bhkd->bhqd', p[..., nb:], vc))

    o = lax.map(row_block, (q_rows, nb_idx, bias))
    o = jnp.moveaxis(o, 0, 2).reshape(b, NA_HEADS, n, NA_HEAD_DIM)
    yx = merge_heads(o) @ w_o
    if not need_ctx:
        return yx, None
    s = jnp.einsum('bhqd,bhkd->bhqk', qc, kc) * NA_SCALE
    p = jax.nn.softmax(s.astype(F32), axis=-1).astype(vc.dtype)
    yc = merge_heads(jnp.einsum('bhqk,bhkd->bhqd', p, vc)) @ w_o
    return yx, yc


def retention_scan(q, k, v, log_gamma, state):
    b, h, n, _ = q.shape
    dv = v.shape[-1]
    nc = n // RET_CHUNK

    def chunks(t):
        return jnp.moveaxis(t.reshape(b, h, nc, RET_CHUNK, t.shape[-1]), 2, 0)

    pos = jnp.arange(RET_CHUNK, dtype=F32)
    lg = log_gamma.astype(F32)[:, None, None]
    diff = pos[:, None] - pos[None, :]
    intra = jnp.where(diff >= 0, jnp.exp(jnp.maximum(diff, 0.0)[None] * lg), 0.0)
    q_decay = jnp.exp((pos + 1.0)[None, :, None] * lg)
    k_decay = jnp.exp((RET_CHUNK - 1.0 - pos)[None, :, None] * lg)
    chunk_decay = jnp.exp(RET_CHUNK * lg)

    def step(s, inp):
        qc, kc, vc = inp
        v32 = vc.astype(F32)
        scores = jnp.einsum('bhid,bhjd->bhij', qc, kc).astype(F32) * intra
        out = (jnp.einsum('bhij,bhjv->bhiv', scores, v32)
               + jnp.einsum('bhid,bhdv->bhiv', qc.astype(F32) * q_decay, s))
        s = s * chunk_decay + jnp.einsum('bhjd,bhjv->bhdv', kc.astype(F32) * k_decay, v32)
        return s, out

    state, out = lax.scan(step, state, (chunks(q), chunks(k), chunks(v)))
    return jnp.moveaxis(out, 0, 2).reshape(b, h, n, dv), state


def retention_mixer(hx, hc, w_q, w_k, w_v, w_g, w_o, gn_gain, decay_logit, cos, sin, need_ctx):
    b = hx.shape[0]
    k_scale = RET_QK_DIM ** -0.5
    qx = apply_rope(split_heads(hx @ w_q, RET_HEADS), cos, sin)
    kx = apply_rope(split_heads(hx @ w_k, RET_HEADS), cos, sin) * k_scale
    vx = split_heads(hx @ w_v, RET_HEADS)
    qc = split_heads(hc @ w_q, RET_HEADS)
    kc = split_heads(hc @ w_k, RET_HEADS) * k_scale
    vc = split_heads(hc @ w_v, RET_HEADS)
    log_gamma = jax.nn.log_sigmoid(decay_logit.astype(F32))
    zero = jnp.zeros((b, RET_HEADS, RET_QK_DIM, RET_V_DIM), F32)

    def flip(t):
        return jnp.flip(t, axis=2)

    oc_f, sc_f = retention_scan(qc, kc, vc, log_gamma[0], zero)
    ox_f, _ = retention_scan(qx, kx, vx, log_gamma[0], sc_f)
    oc_b, sc_b = retention_scan(flip(qc), flip(kc), flip(vc), log_gamma[1], zero)
    ox_b, _ = retention_scan(flip(qx), flip(kx), flip(vx), log_gamma[1], sc_b)

    def finish(h, o):
        mu = jnp.mean(o, axis=-1, keepdims=True)
        oc_ = o - mu
        o = oc_ * lax.rsqrt(jnp.mean(oc_ * oc_, axis=-1, keepdims=True) + EPS)
        o = (merge_heads(o) * gn_gain.astype(F32)).astype(h.dtype)
        return (jax.nn.silu(h @ w_g) * o) @ w_o

    yx = finish(hx, ox_f + flip(ox_b))
    if not need_ctx:
        return yx, None
    return yx, finish(hc, oc_f + flip(oc_b))


def conv_mixer(hx, hc, w_pw1, b_pw1, w_dw, b_dw, ln_g, ln_b, w_pw2, b_pw2, need_ctx):
    def one(h):
        a, g = jnp.split(h @ w_pw1 + b_pw1, 2, axis=-1)
        u = a * jax.nn.sigmoid(g)
        u = depthwise_conv(u, w_dw, b_dw)
        u = jax.nn.silu(layer_norm(u, ln_g, ln_b))
        return u @ w_pw2 + b_pw2
    yx = one(hx)
    if not need_ctx:
        return yx, None
    return yx, one(hc)


def conv_ffn(h, w_up, w_dw, b_dw, w_down):
    u, v = jnp.split(h @ w_up, 2, axis=-1)
    u = jax.nn.gelu(depthwise_conv(u, w_dw, b_dw))
    return (u * v) @ w_down


def setup_inputs(seed: int = 0) -> dict:
    key = jax.random.key(seed)
    ks = iter(jax.random.split(key, 48))
    D = D_MODEL

    def nrm(shape, scale):
        return jax.random.normal(next(ks), shape, F32) * scale

    base_logit = jnp.log(jnp.power(2.0, 5.0 + jnp.arange(RET_HEADS, dtype=F32)) - 1.0)
    return {
        'x': nrm((BATCH, SEQ, D), 1.0),
        'c': nrm((BATCH, D), 1.0),
        'ctx': nrm((BATCH, CTX_LEN, D), 1.0),
        'c_ctx': nrm((D,), 1.0),
        'ada_w': nrm((DEPTH, D, 6 * D), D ** -0.5),
        'ada_b': nrm((DEPTH, 6 * D), 0.02),
        'norm_mix': 1.0 + nrm((DEPTH, D), 0.1),
        'norm_ffn': 1.0 + nrm((DEPTH, D), 0.1),
        'na_w_qkv': nrm((N_NA_LAYERS, D, 3 * D), D ** -0.5),
        'na_w_o': nrm((N_NA_LAYERS, D, D), D ** -0.5),
        'na_q_gain': 1.0 + nrm((N_NA_LAYERS, NA_HEAD_DIM), 0.1),
        'na_k_gain': 1.0 + nrm((N_NA_LAYERS, NA_HEAD_DIM), 0.1),
        'na_rpb': nrm((N_NA_LAYERS, NA_HEADS, 2 * NA_KH - 1, 2 * NA_KW - 1), 0.1),
        'ret_w_q': nrm((N_RET_LAYERS, D, RET_HEADS * RET_QK_DIM), D ** -0.5),
        'ret_w_k': nrm((N_RET_LAYERS, D, RET_HEADS * RET_QK_DIM), D ** -0.5),
        'ret_w_v': nrm((N_RET_LAYERS, D, RET_HEADS * RET_V_DIM), D ** -0.5),
        'ret_w_g': nrm((N_RET_LAYERS, D, RET_HEADS * RET_V_DIM), D ** -0.5),
        'ret_w_o': nrm((N_RET_LAYERS, RET_HEADS * RET_V_DIM, D), (RET_HEADS * RET_V_DIM) ** -0.5),
        'ret_gn_gain': 1.0 + nrm((N_RET_LAYERS, RET_HEADS * RET_V_DIM), 0.1),
        'ret_decay_logit': base_logit + nrm((N_RET_LAYERS, 2, RET_HEADS), 0.1),
        'cv_w_pw1': nrm((N_CONV_LAYERS, D, 2 * D), D ** -0.5),
        'cv_b_pw1': nrm((N_CONV_LAYERS, 2 * D), 0.02),
        'cv_w_dw': nrm((N_CONV_LAYERS, CONV_WIDTH, D), CONV_WIDTH ** -0.5),
        'cv_b_dw': nrm((N_CONV_LAYERS, D), 0.02),
        'cv_ln_g': 1.0 + nrm((N_CONV_LAYERS, D), 0.1),
        'cv_ln_b': nrm((N_CONV_LAYERS, D), 0.02),
        'cv_w_pw2': nrm((N_CONV_LAYERS, D, D), D ** -0.5),
        'cv_b_pw2': nrm((N_CONV_LAYERS, D), 0.02),
        'ffn_w_up': nrm((DEPTH, D, 2 * FFN_HIDDEN), D ** -0.5),
        'ffn_w_dw': nrm((DEPTH, FFN_CONV_WIDTH, FFN_HIDDEN), FFN_CONV_WIDTH ** -0.5),
        'ffn_b_dw': nrm((DEPTH, FFN_HIDDEN), 0.02),
        'ffn_w_down': nrm((DEPTH, FFN_HIDDEN, D), FFN_HIDDEN ** -0.5),
    }


def reference(x, c, ctx, c_ctx, ada_w, ada_b, norm_mix, norm_ffn,
              na_w_qkv, na_w_o, na_q_gain, na_k_gain, na_rpb,
              ret_w_q, ret_w_k, ret_w_v, ret_w_g, ret_w_o, ret_gn_gain, ret_decay_logit,
              cv_w_pw1, cv_b_pw1, cv_w_dw, cv_b_dw, cv_ln_g, cv_ln_b, cv_w_pw2, cv_b_pw2,
              ffn_w_up, ffn_w_dw, ffn_b_dw, ffn_w_down):
    n = x.shape[1]
    cos, sin = axial_rope_tables(n, RET_QK_DIM)
    silu_c = jax.nn.silu(c)
    silu_cc = jax.nn.silu(c_ctx)[None, :]
    xs, cs = x, ctx
    for i in range(DEPTH):
        need_ctx = i < DEPTH - 1
        mx = jnp.split((silu_c @ ada_w[i] + ada_b[i])[:, None, :], 6, axis=-1)
        mc = jnp.split((silu_cc @ ada_w[i] + ada_b[i])[:, None, :], 6, axis=-1)
        hx = modulate_norm(xs, norm_mix[i], mx[0], mx[1])
        hc = modulate_norm(cs, norm_mix[i], mc[0], mc[1])
        kind, j = i % N_MIXERS, i // N_MIXERS
        if kind == 0:
            yx, yc = na_mixer(hx, hc, na_w_qkv[j], na_w_o[j], na_q_gain[j], na_k_gain[j], na_rpb[j], need_ctx)
        elif kind == 1:
            yx, yc = retention_mixer(hx, hc, ret_w_q[j], ret_w_k[j], ret_w_v[j], ret_w_g[j], ret_w_o[j],
                                     ret_gn_gain[j], ret_decay_logit[j], cos, sin, need_ctx)
        else:
            yx, yc = conv_mixer(hx, hc, cv_w_pw1[j], cv_b_pw1[j], cv_w_dw[j], cv_b_dw[j],
                                cv_ln_g[j], cv_ln_b[j], cv_w_pw2[j], cv_b_pw2[j], need_ctx)
        xs = xs + mx[2] * yx
        hx = modulate_norm(xs, norm_ffn[i], mx[3], mx[4])
        xs = xs + mx[5] * conv_ffn(hx, ffn_w_up[i], ffn_w_dw[i], ffn_b_dw[i], ffn_w_down[i])
        if need_ctx:
            cs = cs + mc[2] * yc
            hc = modulate_norm(cs, norm_ffn[i], mc[3], mc[4])
            cs = cs + mc[5] * conv_ffn(hc, ffn_w_up[i], ffn_w_dw[i], ffn_b_dw[i], ffn_w_down[i])
    return xs
```

```python
import functools

import numpy as np
import jax
import jax.numpy as jnp
from jax import lax
from jax.experimental import pallas as pl
from jax.experimental.pallas import tpu as pltpu

D = 1024
BATCH = 4
SEQ = 4096
DEPTH = 4
CTX = 256
GRID_W = 64
ROWS = SEQ // GRID_W
EPS = 1e-6

NA_HEADS = 16
NA_DH = 64
NA_KH = 8
NA_KW = 16
NA_SCALE = NA_DH ** -0.5
NA_PAIRS = NA_HEADS // 2
NA_WIN = NA_KH * GRID_W

RET_HEADS = 4
RET_DK = 256
RET_DV = 512
RET_C = 128
ROPE_BASE = 10000.0

CONV_W = 31
CONV_HALO = 16
FFN_H = 2816
FFN_CHUNKS = 2
FFN_TH = FFN_H // FFN_CHUNKS

MOD_ROWS = 8
CTX_ROW = BATCH
NEG = -1e30

F32 = jnp.float32
BF16 = jnp.bfloat16

VMEM_LIMIT = 52 * 1024 * 1024


def _params(*sem):
    return pltpu.CompilerParams(dimension_semantics=sem, vmem_limit_bytes=VMEM_LIMIT)


def _mod_norm(x, gain, shift, scale):
    ms = jnp.mean(x * x, axis=-1, keepdims=True)
    return x * lax.rsqrt(ms + EPS) * gain * (1.0 + scale) + shift


def _dot(a, b):
    return jnp.dot(a, b, preferred_element_type=F32)


def _dot_nt(a, b):
    return lax.dot_general(a, b, (((1,), (1,)), ((), ())), preferred_element_type=F32)


def _dot_tn(a, b):
    return lax.dot_general(a, b, (((0,), (0,)), ((), ())), preferred_element_type=F32)


def _ada_kernel(c_ref, w_ref, b_ref, o_ref):
    c = c_ref[...]
    s = (c * jax.nn.sigmoid(c)).astype(BF16)
    o_ref[0] = _dot(s, w_ref[0].astype(BF16)) + b_ref[0]


def ada_table(c_rows, ada_w, ada_b):
    tn = 1536
    return pl.pallas_call(
        _ada_kernel,
        grid=(DEPTH, 6 * D // tn),
        in_specs=[
            pl.BlockSpec((MOD_ROWS, D), lambda l, j: (0, 0)),
            pl.BlockSpec((1, D, tn), lambda l, j: (l, 0, j)),
            pl.BlockSpec((1, 1, tn), lambda l, j: (l, 0, j)),
        ],
        out_specs=pl.BlockSpec((1, MOD_ROWS, tn), lambda l, j: (l, 0, j)),
        out_shape=jax.ShapeDtypeStruct((DEPTH, MOD_ROWS, 6 * D), F32),
        compiler_params=_params("parallel", "parallel"),
        name="ada_table",
    )(c_rows, ada_w, ada_b.reshape(DEPTH, 1, 6 * D))


def _mod_spec(row_of):
    return pl.BlockSpec((1, 1, D), lambda i, *_: (row_of(i), 0, 0))


def _vec_spec(n):
    return pl.BlockSpec((1, n), lambda i, *_: (0, 0))


def _na_qkv_kernel(x_ref, sh_ref, sc_ref, gain_ref, w_ref, qg_ref, kg_ref, o_ref):
    h = _mod_norm(x_ref[...], gain_ref[...], sh_ref[0], sc_ref[0]).astype(BF16)
    r = lax.broadcasted_iota(jnp.int32, (128, 128), 0) // NA_DH
    c = lax.broadcasted_iota(jnp.int32, (128, 128), 1) // NA_DH
    seg = jnp.where(r == c, 1.0, 0.0).astype(BF16)
    for sec, g_ref in ((0, qg_ref), (1, kg_ref)):
        y = _dot(h, w_ref[:, sec * D:(sec + 1) * D])
        for g in range(D // 128):
            yg = y[:, g * 128:(g + 1) * 128]
            ss = _dot((yg * yg).astype(BF16), seg)
            yn = yg * lax.rsqrt(ss * (1.0 / NA_DH) + EPS) * g_ref[:, g * 128:(g + 1) * 128]
            o_ref[:, sec * D + g * 128: sec * D + (g + 1) * 128] = yn.astype(o_ref.dtype)
    o_ref[:, 2 * D:] = _dot(h, w_ref[:, 2 * D:]).astype(o_ref.dtype)


def na_qkv(x, sh, sc, gain, w, qg, kg, *, tm, row_of):
    m = x.shape[0]
    return pl.pallas_call(
        _na_qkv_kernel,
        grid=(m // tm,),
        in_specs=[
            pl.BlockSpec((tm, D), lambda i: (i, 0)),
            _mod_spec(row_of), _mod_spec(row_of), _vec_spec(D),
            pl.BlockSpec((D, 3 * D), lambda i: (0, 0)),
            _vec_spec(D), _vec_spec(D),
        ],
        out_specs=pl.BlockSpec((tm, 3 * D), lambda i: (i, 0)),
        out_shape=jax.ShapeDtypeStruct((m, 3 * D), BF16),
        compiler_params=_params("parallel"),
        name="na_qkv",
    )(x, sh, sc, gain, w, qg, kg)


def _split_heads(q, in_a):
    zero = jnp.zeros_like(q)
    return jnp.concatenate([jnp.where(in_a, q, zero), jnp.where(in_a, zero, q)], axis=0)


def _na_attn_kernel(q_ref, k_ref, v_ref, kc_ref, vc_ref, bias_ref, *rest, need_ctx):
    if need_ctx:
        qc_ref, o_ref, oc_ref = rest
    else:
        (o_ref,) = rest
    in_a = lax.broadcasted_iota(jnp.int32, (1, 128), 1) < NA_DH
    kc = kc_ref[...]
    vc = vc_ref[...]

    def row(r, carry):
        rs = jnp.clip(r - NA_KH // 2, 0, ROWS - NA_KH)
        q0 = pl.multiple_of(r * GRID_W, GRID_W)
        k0 = pl.multiple_of(rs * GRID_W, GRID_W)
        q2 = _split_heads(q_ref[pl.ds(q0, GRID_W), :], in_a)
        s_loc = _dot_nt(q2, k_ref[pl.ds(k0, NA_WIN), :]) + bias_ref[0, r - rs]
        s_ctx = _dot_nt(q2, kc)
        m = jnp.maximum(jnp.max(s_loc, axis=-1, keepdims=True), jnp.max(s_ctx, axis=-1, keepdims=True))
        p_loc = jnp.exp(s_loc - m)
        p_ctx = jnp.exp(s_ctx - m)
        l = jnp.sum(p_loc, axis=-1, keepdims=True) + jnp.sum(p_ctx, axis=-1, keepdims=True)
        o2 = _dot(p_loc.astype(BF16), v_ref[pl.ds(k0, NA_WIN), :]) + _dot(p_ctx.astype(BF16), vc)
        o2 = o2 / l
        o = jnp.where(in_a, o2[:GRID_W], o2[GRID_W:])
        o_ref[pl.ds(q0, GRID_W), :] = o.astype(o_ref.dtype)
        return carry

    lax.fori_loop(0, ROWS, row, 0)

    if need_ctx:
        q2 = _split_heads(qc_ref[...], in_a)
        s = _dot_nt(q2, kc)
        p = jnp.exp(s - jnp.max(s, axis=-1, keepdims=True))
        o2 = _dot(p.astype(BF16), vc) / jnp.sum(p, axis=-1, keepdims=True)
        oc_ref[...] = jnp.where(in_a, o2[:CTX], o2[CTX:]).astype(oc_ref.dtype)


def na_attention(qkv_x, qkv_c, bias, *, need_ctx):
    def col(sec):
        return lambda p, b: (b, sec * NA_PAIRS + p)
    in_specs = [
        pl.BlockSpec((SEQ, 128), col(0)),
        pl.BlockSpec((SEQ, 128), col(1)),
        pl.BlockSpec((SEQ, 128), col(2)),
        pl.BlockSpec((CTX, 128), col(1)),
        pl.BlockSpec((CTX, 128), col(2)),
        pl.BlockSpec((1, NA_KH, 128, NA_WIN), lambda p, b: (p, 0, 0, 0)),
    ]
    args = [qkv_x, qkv_x, qkv_x, qkv_c, qkv_c, bias]
    out_specs = [pl.BlockSpec((SEQ, 128), lambda p, b: (b, p))]
    out_shape = [jax.ShapeDtypeStruct((BATCH * SEQ, D), BF16)]
    if need_ctx:
        in_specs.append(pl.BlockSpec((CTX, 128), col(0)))
        args.append(qkv_c)
        out_specs.append(pl.BlockSpec((CTX, 128), lambda p, b: (b, p)))
        out_shape.append(jax.ShapeDtypeStruct((BATCH * CTX, D), BF16))
    out = pl.pallas_call(
        functools.partial(_na_attn_kernel, need_ctx=need_ctx),
        grid=(NA_PAIRS, BATCH),
        in_specs=in_specs,
        out_specs=out_specs,
        out_shape=out_shape,
        compiler_params=_params("parallel", "parallel"),
        name="na_attention",
    )(*args)
    return (out[0], out[1]) if need_ctx else (out[0], None)


def na_bias_table(rpb):
    var = np.arange(NA_KH)[:, None]
    i = np.arange(NA_KH)[None, :]
    rel_r = i - var + (NA_KH - 1)
    c = np.arange(GRID_W)[:, None]
    kc = np.arange(GRID_W)[None, :]
    cs = np.clip(c - NA_KW // 2, 0, GRID_W - NA_KW)
    valid = (kc >= cs) & (kc < cs + NA_KW)
    rel_c = np.clip(kc - c + (NA_KW - 1), 0, 2 * NA_KW - 2)
    t = rpb[:, rel_r[:, :, None, None], rel_c[None, None, :, :]]
    t = jnp.where(valid[None, None, None], t, NEG)
    t = t.transpose(0, 1, 3, 2, 4).reshape(NA_PAIRS, 2, NA_KH, GRID_W, NA_WIN)
    return t.transpose(0, 2, 1, 3, 4).reshape(NA_PAIRS, NA_KH, 2 * GRID_W, NA_WIN)


def _proj_res_kernel(a_ref, w_ref, b_ref, gate_ref, res_ref, o_ref):
    y = _dot(a_ref[...], w_ref[...]) + b_ref[...]
    o_ref[...] = res_ref[...] + gate_ref[0] * y


def proj_residual(a, w, b, gate, res, *, tm, row_of):
    m, k = a.shape
    return pl.pallas_call(
        _proj_res_kernel,
        grid=(m // tm,),
        in_specs=[
            pl.BlockSpec((tm, k), lambda i: (i, 0)),
            pl.BlockSpec((k, D), lambda i: (0, 0)),
            _vec_spec(D), _mod_spec(row_of),
            pl.BlockSpec((tm, D), lambda i: (i, 0)),
        ],
        out_specs=pl.BlockSpec((tm, D), lambda i: (i, 0)),
        out_shape=jax.ShapeDtypeStruct((m, D), F32),
        compiler_params=_params("parallel"),
        name="proj_residual",
    )(a, w, b, gate, res)


def _ffn_kernel(xp_ref, x_ref, xn_ref, sh_ref, sc_ref, gate_ref, gain_ref, wu_ref, wv_ref, cw_ref, cb_ref,
                wd_ref, o_ref, h_scr, acc_scr, *, tm, tiles_per_seq):
    i = pl.program_id(0)
    j = pl.program_id(1)
    hl = 16

    @pl.when(j == 0)
    def _():
        gain, shift, scale = gain_ref[...], sh_ref[0], sc_ref[0]
        t = i % tiles_per_seq
        hp = _mod_norm(xp_ref[...], gain, shift, scale)
        hn = _mod_norm(xn_ref[...], gain, shift, scale)
        h_scr[0:hl] = jnp.where(t == 0, 0.0, hp).astype(BF16)
        h_scr[hl:hl + tm] = _mod_norm(x_ref[...], gain, shift, scale).astype(BF16)
        h_scr[hl + tm:] = jnp.where(t == tiles_per_seq - 1, 0.0, hn).astype(BF16)

    u = _dot(h_scr[...], wu_ref[...])
    v = _dot(h_scr[hl:hl + tm], wv_ref[...])
    cw = cw_ref[0]
    uc = (cw[0:1] * u[hl - 1:hl - 1 + tm] + cw[1:2] * u[hl:hl + tm] + cw[2:3] * u[hl + 1:hl + 1 + tm]
          + cb_ref[0])
    g = (jax.nn.gelu(uc, approximate=True) * v).astype(BF16)
    part = _dot(g, wd_ref[...])

    @pl.when(j == 0)
    def _():
        acc_scr[...] = part

    @pl.when(j > 0)
    def _():
        acc_scr[...] += part

    @pl.when(j == FFN_CHUNKS - 1)
    def _():
        o_ref[...] = x_ref[...] + gate_ref[0] * acc_scr[...]


def conv_ffn(x, sh, sc, gate, gain, wu, wv, cw, cb, wd, *, tm, seq, row_of):
    m = x.shape[0]
    hb = tm // 16
    last_hb = m // 16 - 1
    return pl.pallas_call(
        functools.partial(_ffn_kernel, tm=tm, tiles_per_seq=seq // tm),
        grid=(m // tm, FFN_CHUNKS),
        in_specs=[
            pl.BlockSpec((16, D), lambda i, j: (jnp.maximum(i * hb - 1, 0), 0)),
            pl.BlockSpec((tm, D), lambda i, j: (i, 0)),
            pl.BlockSpec((16, D), lambda i, j: (jnp.minimum((i + 1) * hb, last_hb), 0)),
            _mod_spec(row_of), _mod_spec(row_of), _mod_spec(row_of), _vec_spec(D),
            pl.BlockSpec((D, FFN_TH), lambda i, j: (0, j)),
            pl.BlockSpec((D, FFN_TH), lambda i, j: (0, j)),
            pl.BlockSpec((1, 3, FFN_TH), lambda i, j: (j, 0, 0)),
            pl.BlockSpec((1, 1, FFN_TH), lambda i, j: (j, 0, 0)),
            pl.BlockSpec((FFN_TH, D), lambda i, j: (j, 0)),
        ],
        out_specs=pl.BlockSpec((tm, D), lambda i, j: (i, 0)),
        out_shape=jax.ShapeDtypeStruct((m, D), F32),
        scratch_shapes=[pltpu.VMEM((tm + 32, D), BF16), pltpu.VMEM((tm, D), F32)],
        compiler_params=_params("parallel", "arbitrary"),
        name="conv_ffn",
    )(x, x, x, sh, sc, gate, gain, wu, wv, cw, cb, wd)


def _ret_proj_kernel(x_ref, sh_ref, sc_ref, gain_ref, w_ref, cos_ref, sin_ref, o_ref, h_scr, *, rope):
    j = pl.program_id(1)

    @pl.when(j == 0)
    def _():
        h_scr[...] = _mod_norm(x_ref[...], gain_ref[...], sh_ref[0], sc_ref[0]).astype(BF16)

    y = _dot(h_scr[...], w_ref[...])

    @pl.when(j == 0)
    def _():
        k_scale = RET_DK ** -0.5
        for hh in range(2 * RET_HEADS):
            s = k_scale if hh >= RET_HEADS else 1.0
            x1 = y[:, hh * RET_DK: hh * RET_DK + 128]
            x2 = y[:, hh * RET_DK + 128: (hh + 1) * RET_DK]
            if rope:
                cos, sin = cos_ref[...], sin_ref[...]
                x1, x2 = x1 * cos - x2 * sin, x1 * sin + x2 * cos
            o_ref[:, hh * RET_DK: hh * RET_DK + 128] = (x1 * s).astype(o_ref.dtype)
            o_ref[:, hh * RET_DK + 128: (hh + 1) * RET_DK] = (x2 * s).astype(o_ref.dtype)

    @pl.when(j == 1)
    def _():
        o_ref[...] = y.astype(o_ref.dtype)

    @pl.when(j == 2)
    def _():
        o_ref[...] = (y * jax.nn.sigmoid(y)).astype(o_ref.dtype)


def ret_proj(x, sh, sc, gain, w, cos, sin, *, tm, seq, row_of, rope):
    m = x.shape[0]
    tn = 2 * D
    tps = seq // tm
    return pl.pallas_call(
        functools.partial(_ret_proj_kernel, rope=rope),
        grid=(m // tm, 3),
        in_specs=[
            pl.BlockSpec((tm, D), lambda i, j: (i, 0)),
            _mod_spec(row_of), _mod_spec(row_of), _vec_spec(D),
            pl.BlockSpec((D, tn), lambda i, j: (0, j)),
            pl.BlockSpec((tm, 128), lambda i, j: (i % tps, 0)),
            pl.BlockSpec((tm, 128), lambda i, j: (i % tps, 0)),
        ],
        out_specs=pl.BlockSpec((tm, tn), lambda i, j: (i, j)),
        out_shape=jax.ShapeDtypeStruct((m, 3 * tn), BF16),
        scratch_shapes=[pltpu.VMEM((tm, D), BF16)],
        compiler_params=_params("parallel", "arbitrary"),
        name="ret_proj",
    )(x, sh, sc, gain, w, cos, sin)


def _log_sigmoid(x):
    return jnp.minimum(x, 0.0) - jnp.log1p(jnp.exp(-jnp.abs(x)))


def _ret_scan_kernel(qx_ref, kx_ref, vx_ref, qc_ref, kc_ref, vc_ref, dl_ref, ox_ref, oc_ref,
                     s_scr, ax_scr, ac_scr):
    C = RET_C
    pos = lax.broadcasted_iota(jnp.int32, (C, 128), 0).astype(F32)
    ii = lax.broadcasted_iota(jnp.int32, (C, C), 0).astype(F32)
    jj = lax.broadcasted_iota(jnp.int32, (C, C), 1).astype(F32)

    for direction in range(2):
        lg = jnp.broadcast_to(_log_sigmoid(dl_ref[direction, 0])[0:1, :], (C, 128))
        if direction == 0:
            diff = ii - jj
            p = pos
        else:
            diff = jj - ii
            p = (C - 1.0) - pos
        intra = jnp.where(diff >= 0, jnp.exp(jnp.maximum(diff, 0.0) * lg), 0.0)
        qd = jnp.exp((p + 1.0) * lg)
        kd = jnp.exp((C - 1.0 - p) * lg)
        cd = jnp.exp(C * lg)[0:1, :]
        qd_v = jnp.concatenate([qd] * (RET_DV // 128), axis=1)
        kd_k = jnp.concatenate([kd] * (RET_DK // 128), axis=1)
        cd_v = jnp.concatenate([cd] * (RET_DV // 128), axis=1)
        s_scr[...] = jnp.zeros_like(s_scr)

        def chunk(q_ref, k_ref, v_ref, a_ref, start):
            q = q_ref[pl.ds(start, C), :]
            k = k_ref[pl.ds(start, C), :]
            v = v_ref[pl.ds(start, C), :]
            scores = (_dot_nt(q, k) * intra).astype(BF16)
            s_old = s_scr[...]
            o = _dot(scores, v) + qd_v * _dot(q, s_old.astype(BF16))
            if direction == 0:
                a_ref[pl.ds(start, C), :] = o
            else:
                a_ref[pl.ds(start, C), :] += o
            kdec = (k.astype(F32) * kd_k).astype(BF16)
            s_scr[...] = s_old * cd_v + _dot_tn(kdec, v)

        n_c = CTX // C
        n_x = SEQ // C
        for t in range(n_c):
            cidx = t if direction == 0 else n_c - 1 - t
            chunk(qc_ref, kc_ref, vc_ref, ac_scr, cidx * C)

        def x_chunk(t, carry):
            cidx = t if direction == 0 else n_x - 1 - t
            chunk(qx_ref, kx_ref, vx_ref, ax_scr, pl.multiple_of(cidx * C, C))
            return carry

        lax.fori_loop(0, n_x, x_chunk, 0)

    def group_norm(o):
        mu = jnp.mean(o, axis=-1, keepdims=True)
        oc = o - mu
        return oc * lax.rsqrt(jnp.mean(oc * oc, axis=-1, keepdims=True) + EPS)

    oc_ref[...] = group_norm(ac_scr[...]).astype(oc_ref.dtype)
    rb = 512

    def gn_block(t, carry):
        r0 = pl.multiple_of(t * rb, rb)
        ox_ref[pl.ds(r0, rb), :] = group_norm(ax_scr[pl.ds(r0, rb), :]).astype(ox_ref.dtype)
        return carry

    lax.fori_loop(0, SEQ // rb, gn_block, 0)


def ret_scan(px, pc, dl_tiles):
    kb = RET_HEADS
    vb = 2 * RET_HEADS * RET_DK // RET_DV
    return pl.pallas_call(
        _ret_scan_kernel,
        grid=(BATCH, RET_HEADS),
        in_specs=[
            pl.BlockSpec((SEQ, RET_DK), lambda b, h: (b, h)),
            pl.BlockSpec((SEQ, RET_DK), lambda b, h: (b, kb + h)),
            pl.BlockSpec((SEQ, RET_DV), lambda b, h: (b, vb + h)),
            pl.BlockSpec((CTX, RET_DK), lambda b, h: (b, h)),
            pl.BlockSpec((CTX, RET_DK), lambda b, h: (b, kb + h)),
            pl.BlockSpec((CTX, RET_DV), lambda b, h: (b, vb + h)),
            pl.BlockSpec((2, 1, 8, 128), lambda b, h: (0, h, 0, 0)),
        ],
        out_specs=[
            pl.BlockSpec((SEQ, RET_DV), lambda b, h: (b, h)),
            pl.BlockSpec((CTX, RET_DV), lambda b, h: (b, h)),
        ],
        out_shape=[
            jax.ShapeDtypeStruct((BATCH * SEQ, RET_HEADS * RET_DV), BF16),
            jax.ShapeDtypeStruct((BATCH * CTX, RET_HEADS * RET_DV), BF16),
        ],
        scratch_shapes=[
            pltpu.VMEM((RET_DK, RET_DV), F32),
            pltpu.VMEM((SEQ, RET_DV), F32),
            pltpu.VMEM((CTX, RET_DV), F32),
        ],
        compiler_params=_params("parallel", "parallel"),
        name="ret_scan",
    )(px, px, px, pc, pc, pc, dl_tiles)


def _ret_out_kernel(o_ref, g_ref, gn_ref, w_ref, gate_ref, res_ref, out_ref):
    a = (o_ref[...].astype(F32) * gn_ref[...] * g_ref[...].astype(F32)).astype(BF16)
    out_ref[...] = res_ref[...] + gate_ref[0] * _dot(a, w_ref[...])


def ret_out(o, proj, gn, w, gate, res, *, tm, row_of):
    m = o.shape[0]
    dv = RET_HEADS * RET_DV
    return pl.pallas_call(
        _ret_out_kernel,
        grid=(m // tm,),
        in_specs=[
            pl.BlockSpec((tm, dv), lambda i: (i, 0)),
            pl.BlockSpec((tm, dv), lambda i: (i, 2)),
            _vec_spec(dv),
            pl.BlockSpec((dv, D), lambda i: (0, 0)),
            _mod_spec(row_of),
            pl.BlockSpec((tm, D), lambda i: (i, 0)),
        ],
        out_specs=pl.BlockSpec((tm, D), lambda i: (i, 0)),
        out_shape=jax.ShapeDtypeStruct((m, D), F32),
        compiler_params=_params("parallel"),
        name="ret_out",
    )(o, proj, gn, w, gate, res)


def _pw1_glu_kernel(x_ref, sh_ref, sc_ref, gain_ref, w_ref, b_ref, o_ref):
    h = _mod_norm(x_ref[...], gain_ref[...], sh_ref[0], sc_ref[0]).astype(BF16)
    y = _dot(h, w_ref[...]) + b_ref[...]
    o_ref[...] = y[:, :D] * jax.nn.sigmoid(y[:, D:])


def pw1_glu(x, sh, sc, gain, w, b, *, tm, row_of):
    m = x.shape[0]
    return pl.pallas_call(
        _pw1_glu_kernel,
        grid=(m // tm,),
        in_specs=[
            pl.BlockSpec((tm, D), lambda i: (i, 0)),
            _mod_spec(row_of), _mod_spec(row_of), _vec_spec(D),
            pl.BlockSpec((D, 2 * D), lambda i: (0, 0)),
            _vec_spec(2 * D),
        ],
        out_specs=pl.BlockSpec((tm, D), lambda i: (i, 0)),
        out_shape=jax.ShapeDtypeStruct((m, D), F32),
        compiler_params=_params("parallel"),
        name="pw1_glu",
    )(x, sh, sc, gain, w, b)


def _dwconv_kernel(up_ref, u_ref, un_ref, cw_ref, cb_ref, lg_ref, lb_ref, w_ref, b_ref, gate_ref, res_ref,
                   o_ref, ue_scr, *, tm, tiles_per_seq):
    t = pl.program_id(0) % tiles_per_seq
    hl = CONV_HALO
    ue_scr[0:hl] = jnp.where(t == 0, 0.0, up_ref[...])
    ue_scr[hl:hl + tm] = u_ref[...]
    ue_scr[hl + tm:] = jnp.where(t == tiles_per_seq - 1, 0.0, un_ref[...])
    base = hl - CONV_W // 2
    span = tm + 8 * ((base + CONV_W - 1) // 8)
    acc = jnp.zeros((tm, D), F32)
    for b in range(8):
        ub = ue_scr[pl.ds(b, span), :]
        for a in range((base + CONV_W - 1) // 8 + 1):
            k = 8 * a + b - base
            if 0 <= k < CONV_W:
                acc = acc + cw_ref[k:k + 1, :] * ub[8 * a:8 * a + tm]
    y = acc + cb_ref[...]
    mu = jnp.mean(y, axis=-1, keepdims=True)
    yc = y - mu
    var = jnp.mean(yc * yc, axis=-1, keepdims=True)
    z = yc * lax.rsqrt(var + EPS) * lg_ref[...] + lb_ref[...]
    z = (z * jax.nn.sigmoid(z)).astype(BF16)
    o_ref[...] = res_ref[...] + gate_ref[0] * (_dot(z, w_ref[...]) + b_ref[...])


def dwconv_out(u, cw, cb, lg, lb, w, b, gate, res, *, tm, seq, row_of):
    m = u.shape[0]
    hb = tm // CONV_HALO
    last_hb = m // CONV_HALO - 1
    return pl.pallas_call(
        functools.partial(_dwconv_kernel, tm=tm, tiles_per_seq=seq // tm),
        grid=(m // tm,),
        in_specs=[
            pl.BlockSpec((CONV_HALO, D), lambda i: (jnp.maximum(i * hb - 1, 0), 0)),
            pl.BlockSpec((tm, D), lambda i: (i, 0)),
            pl.BlockSpec((CONV_HALO, D), lambda i: (jnp.minimum((i + 1) * hb, last_hb), 0)),
            pl.BlockSpec((CONV_W, D), lambda i: (0, 0)),
            _vec_spec(D), _vec_spec(D), _vec_spec(D),
            pl.BlockSpec((D, D), lambda i: (0, 0)),
            _vec_spec(D), _mod_spec(row_of),
            pl.BlockSpec((tm, D), lambda i: (i, 0)),
        ],
        out_specs=pl.BlockSpec((tm, D), lambda i: (i, 0)),
        out_shape=jax.ShapeDtypeStruct((m, D), F32),
        scratch_shapes=[pltpu.VMEM((tm + 2 * CONV_HALO, D), F32)],
        compiler_params=_params("parallel"),
        name="dwconv_out",
    )(u, u, u, cw, cb, lg, lb, w, b, gate, res)


def _rope_tables():
    t = np.arange(SEQ)
    row = (t // GRID_W).astype(np.float32)
    col = (t % GRID_W).astype(np.float32)
    n_freq = RET_DK // 4
    inv_freq = jnp.power(ROPE_BASE, -jnp.arange(n_freq, dtype=F32) / n_freq)
    ang = jnp.concatenate([row[:, None] * inv_freq, col[:, None] * inv_freq], axis=-1)
    return jnp.cos(ang), jnp.sin(ang)


def kernel(x, c, ctx, c_ctx, ada_w, ada_b, norm_mix, norm_ffn, na_w_qkv, na_w_o, na_q_gain, na_k_gain, na_rpb, ret_w_q, ret_w_k, ret_w_v, ret_w_g, ret_w_o, ret_gn_gain, ret_decay_logit, cv_w_pw1, cv_b_pw1, cv_w_dw, cv_b_dw, cv_ln_g, cv_ln_b, cv_w_pw2, cv_b_pw2, ffn_w_up, ffn_w_dw, ffn_b_dw, ffn_w_down):
    xs = x.reshape(BATCH * SEQ, D)
    cs = ctx.reshape(BATCH * CTX, D)
    tm_x, tm_c = 512, CTX
    tiles_per_batch = SEQ // tm_x

    def x_row(i):
        return i // tiles_per_batch

    def c_row(i):
        return CTX_ROW

    c_rows = jnp.zeros((MOD_ROWS, D), F32).at[:BATCH].set(c).at[CTX_ROW].set(c_ctx)
    mods = ada_table(c_rows, ada_w, ada_b).reshape(DEPTH, MOD_ROWS, 6, 1, D)
    zero_bias = jnp.zeros((1, D), F32)
    cos, sin = _rope_tables()

    for i in range(DEPTH):
        need_ctx = i < DEPTH - 1
        mod = [mods[i, :, k] for k in range(6)]
        g_mix = norm_mix[i].reshape(1, D)
        g_ffn = norm_ffn[i].reshape(1, D)
        kind, j = i % 3, i // 3
        if kind == 0:
            w = na_w_qkv[j].astype(BF16)
            qg = (jnp.tile(na_q_gain[j], NA_HEADS) * NA_SCALE).reshape(1, D)
            kg = jnp.tile(na_k_gain[j], NA_HEADS).reshape(1, D)
            qkv_x = na_qkv(xs, mod[0], mod[1], g_mix, w, qg, kg, tm=tm_x, row_of=x_row)
            qkv_c = na_qkv(cs, mod[0], mod[1], g_mix, w, qg, kg, tm=tm_c, row_of=c_row)
            ox, oc = na_attention(qkv_x, qkv_c, na_bias_table(na_rpb[j]), need_ctx=need_ctx)
            w_o = na_w_o[j].astype(BF16)
            xs = proj_residual(ox, w_o, zero_bias, mod[2], xs, tm=tm_x, row_of=x_row)
            if need_ctx:
                cs = proj_residual(oc, w_o, zero_bias, mod[2], cs, tm=tm_c, row_of=c_row)
        elif kind == 1:
            w = jnp.concatenate([ret_w_q[j], ret_w_k[j], ret_w_v[j], ret_w_g[j]], axis=1).astype(BF16)
            px = ret_proj(xs, mod[0], mod[1], g_mix, w, cos, sin, tm=tm_x, seq=SEQ, row_of=x_row, rope=True)
            pc = ret_proj(cs, mod[0], mod[1], g_mix, w, cos[:CTX], sin[:CTX], tm=tm_c, seq=CTX, row_of=c_row,
                          rope=False)
            dl = jnp.broadcast_to(ret_decay_logit[j][:, :, None, None], (2, RET_HEADS, 8, 128))
            ox, oc = ret_scan(px, pc, dl)
            w_o = ret_w_o[j].astype(BF16)
            gn = ret_gn_gain[j].reshape(1, RET_HEADS * RET_DV)
            xs = ret_out(ox, px, gn, w_o, mod[2], xs, tm=tm_x, row_of=x_row)
            if need_ctx:
                cs = ret_out(oc, pc, gn, w_o, mod[2], cs, tm=tm_c, row_of=c_row)
        else:
            w1 = cv_w_pw1[j].astype(BF16)
            b1 = cv_b_pw1[j].reshape(1, 2 * D)
            w2 = cv_w_pw2[j].astype(BF16)
            conv_args = (cv_w_dw[j], cv_b_dw[j].reshape(1, D), cv_ln_g[j].reshape(1, D), cv_ln_b[j].reshape(1, D),
                         w2, cv_b_pw2[j].reshape(1, D))
            ux = pw1_glu(xs, mod[0], mod[1], g_mix, w1, b1, tm=tm_x, row_of=x_row)
            xs = dwconv_out(ux, *conv_args, mod[2], xs, tm=256, seq=SEQ, row_of=lambda t: t // (SEQ // 256))
            if need_ctx:
                uc = pw1_glu(cs, mod[0], mod[1], g_mix, w1, b1, tm=tm_c, row_of=c_row)
                cs = dwconv_out(uc, *conv_args, mod[2], cs, tm=tm_c, seq=CTX, row_of=c_row)

        w_up = ffn_w_up[i].astype(BF16)
        wu, wv = w_up[:, :FFN_H], w_up[:, FFN_H:]
        cw = ffn_w_dw[i].reshape(3, FFN_CHUNKS, FFN_TH).transpose(1, 0, 2)
        cb = ffn_b_dw[i].reshape(FFN_CHUNKS, 1, FFN_TH)
        wd = ffn_w_down[i].astype(BF16)
        xs = conv_ffn(xs, mod[3], mod[4], mod[5], g_ffn, wu, wv, cw, cb, wd, tm=tm_x, seq=SEQ, row_of=x_row)
        if need_ctx:
            cs = conv_ffn(cs, mod[3], mod[4], mod[5], g_ffn, wu, wv, cw, cb, wd, tm=tm_c, seq=CTX, row_of=c_row)

    return xs.reshape(BATCH, SEQ, D)
```

```python
import functools

import numpy as np
import jax
import jax.numpy as jnp
from jax import lax
from jax.experimental import pallas as pl
from jax.experimental.pallas import tpu as pltpu

D = 1024
BATCH = 4
SEQ = 4096
DEPTH = 4
CTX = 256
GRID_W = 64
ROWS = SEQ // GRID_W
EPS = 1e-6

NA_HEADS = 16
NA_DH = 64
NA_KH = 8
NA_KW = 16
NA_SCALE = NA_DH ** -0.5
NA_PAIRS = NA_HEADS // 2
NA_WIN = NA_KH * GRID_W

RET_HEADS = 4
RET_DK = 256
RET_DV = 512
RET_C = 128
ROPE_BASE = 10000.0

CONV_W = 31
CONV_HALO = 16
FFN_H = 2816
FFN_CHUNKS = 2
FFN_TH = FFN_H // FFN_CHUNKS

MOD_ROWS = 8
CTX_ROW = BATCH
NEG = -1e30
LOG2E = 1.4426950408889634

F32 = jnp.float32
BF16 = jnp.bfloat16

VMEM_LIMIT = 52 * 1024 * 1024


def _params(*sem):
    return pltpu.CompilerParams(dimension_semantics=sem, vmem_limit_bytes=VMEM_LIMIT)


def _mod_norm(x, gain, shift, scale):
    ms = jnp.mean(x * x, axis=-1, keepdims=True)
    return x * lax.rsqrt(ms + EPS) * gain * (1.0 + scale) + shift


def _dot(a, b):
    return jnp.dot(a, b, preferred_element_type=F32)


def _dot_nt(a, b):
    return lax.dot_general(a, b, (((1,), (1,)), ((), ())), preferred_element_type=F32)


def _dot_tn(a, b):
    return lax.dot_general(a, b, (((0,), (0,)), ((), ())), preferred_element_type=F32)


def _ada_kernel(c_ref, w_ref, b_ref, o_ref):
    c = c_ref[...]
    s = (c * jax.nn.sigmoid(c)).astype(BF16)
    o_ref[0] = _dot(s, w_ref[0].astype(BF16)) + b_ref[0]


def ada_table(c_rows, ada_w, ada_b):
    tn = 1536
    return pl.pallas_call(
        _ada_kernel,
        grid=(DEPTH, 6 * D // tn),
        in_specs=[
            pl.BlockSpec((MOD_ROWS, D), lambda l, j: (0, 0)),
            pl.BlockSpec((1, D, tn), lambda l, j: (l, 0, j)),
            pl.BlockSpec((1, 1, tn), lambda l, j: (l, 0, j)),
        ],
        out_specs=pl.BlockSpec((1, MOD_ROWS, tn), lambda l, j: (l, 0, j)),
        out_shape=jax.ShapeDtypeStruct((DEPTH, MOD_ROWS, 6 * D), F32),
        compiler_params=_params("parallel", "parallel"),
        name="ada_table",
    )(c_rows, ada_w, ada_b.reshape(DEPTH, 1, 6 * D))


def _mod_spec(row_of):
    return pl.BlockSpec((1, 1, D), lambda i, *_: (row_of(i), 0, 0))


def _vec_spec(n):
    return pl.BlockSpec((1, n), lambda i, *_: (0, 0))


def _na_qkv_kernel(x_ref, sh_ref, sc_ref, gain_ref, w_ref, qg_ref, kg_ref, o_ref):
    h = _mod_norm(x_ref[...], gain_ref[...], sh_ref[0], sc_ref[0]).astype(BF16)
    r = lax.broadcasted_iota(jnp.int32, (128, 128), 0) // NA_DH
    c = lax.broadcasted_iota(jnp.int32, (128, 128), 1) // NA_DH
    seg = jnp.where(r == c, 1.0, 0.0).astype(BF16)
    for sec, g_ref in ((0, qg_ref), (1, kg_ref)):
        y = _dot(h, w_ref[:, sec * D:(sec + 1) * D])
        for g in range(D // 128):
            yg = y[:, g * 128:(g + 1) * 128]
            ss = _dot((yg * yg).astype(BF16), seg)
            yn = yg * lax.rsqrt(ss * (1.0 / NA_DH) + EPS) * g_ref[:, g * 128:(g + 1) * 128]
            o_ref[:, sec * D + g * 128: sec * D + (g + 1) * 128] = yn.astype(o_ref.dtype)
    o_ref[:, 2 * D:] = _dot(h, w_ref[:, 2 * D:]).astype(o_ref.dtype)


def na_qkv(x, sh, sc, gain, w, qg, kg, *, tm, row_of):
    m = x.shape[0]
    return pl.pallas_call(
        _na_qkv_kernel,
        grid=(m // tm,),
        in_specs=[
            pl.BlockSpec((tm, D), lambda i: (i, 0)),
            _mod_spec(row_of), _mod_spec(row_of), _vec_spec(D),
            pl.BlockSpec((D, 3 * D), lambda i: (0, 0)),
            _vec_spec(D), _vec_spec(D),
        ],
        out_specs=pl.BlockSpec((tm, 3 * D), lambda i: (i, 0)),
        out_shape=jax.ShapeDtypeStruct((m, 3 * D), BF16),
        compiler_params=_params("parallel"),
        name="na_qkv",
    )(x, sh, sc, gain, w, qg, kg)


def _split_heads(q, in_a):
    zero = jnp.zeros_like(q)
    return jnp.concatenate([jnp.where(in_a, q, zero), jnp.where(in_a, zero, q)], axis=0)


def _na_attn_kernel(q_ref, k_ref, v_ref, kc_ref, vc_ref, bias_ref, *rest, need_ctx):
    if need_ctx:
        qc_ref, o_ref, oc_ref, s_scr, p_scr, l_scr = rest
    else:
        o_ref, s_scr, p_scr, l_scr = rest
    in_a = lax.broadcasted_iota(jnp.int32, (1, 128), 1) < NA_DH
    kc = kc_ref[...]
    vc = vc_ref[...]

    def window(r):
        if isinstance(r, int):
            rs = min(max(r - NA_KH // 2, 0), ROWS - NA_KH)
            return r * GRID_W, rs * GRID_W, r - rs
        rs = jnp.clip(r - NA_KH // 2, 0, ROWS - NA_KH)
        return pl.multiple_of(r * GRID_W, GRID_W), pl.multiple_of(rs * GRID_W, GRID_W), r - rs

    def scores(r, slot):
        q0, k0, var = window(r)
        q2 = _split_heads(q_ref[pl.ds(q0, GRID_W), :], in_a)
        s_scr[slot, :, :NA_WIN] = _dot_nt(q2, k_ref[pl.ds(k0, NA_WIN), :]) + bias_ref[0, var]
        s_scr[slot, :, NA_WIN:] = _dot_nt(q2, kc)

    def softmax(slot):
        s = s_scr[slot]
        p = jnp.exp2(s - jnp.max(s, axis=-1, keepdims=True))
        l_scr[slot] = jnp.broadcast_to(jnp.sum(p, axis=-1, keepdims=True), (2 * GRID_W, 128))
        p_scr[slot] = p.astype(BF16)

    def values(r, slot):
        q0, k0, _ = window(r)
        p = p_scr[slot]
        o2 = (_dot(p[:, :NA_WIN], v_ref[pl.ds(k0, NA_WIN), :]) + _dot(p[:, NA_WIN:], vc)) / l_scr[slot]
        o_ref[pl.ds(q0, GRID_W), :] = jnp.where(in_a, o2[:GRID_W], o2[GRID_W:]).astype(o_ref.dtype)

    def stage_values(t):
        for u in range(2):
            values(2 * t + u, 2 * (t % 2) + u)

    def stage_softmax(t):
        for u in range(2):
            softmax(2 * (t % 2) + u)

    def stage_scores(t):
        for u in range(2):
            scores(2 * t + u, 2 * (t % 2) + u)

    n_pairs = ROWS // 2
    stage_scores(0)
    stage_softmax(0)
    stage_scores(1)

    def step(t, carry):
        stage_values(t - 2)
        stage_softmax(t - 1)
        stage_scores(t)
        return carry

    lax.fori_loop(2, n_pairs, step, 0)
    stage_values(n_pairs - 2)
    stage_softmax(n_pairs - 1)
    stage_values(n_pairs - 1)

    if need_ctx:
        q2 = _split_heads(qc_ref[...], in_a)
        s = _dot_nt(q2, kc)
        p = jnp.exp2(s - jnp.max(s, axis=-1, keepdims=True))
        o2 = _dot(p.astype(BF16), vc) / jnp.sum(p, axis=-1, keepdims=True)
        oc_ref[...] = jnp.where(in_a, o2[:CTX], o2[CTX:]).astype(oc_ref.dtype)


def na_attention(qkv_x, qkv_c, bias, *, need_ctx):
    def col(sec):
        return lambda p, b: (b, sec * NA_PAIRS + p)
    in_specs = [
        pl.BlockSpec((SEQ, 128), col(0)),
        pl.BlockSpec((SEQ, 128), col(1)),
        pl.BlockSpec((SEQ, 128), col(2)),
        pl.BlockSpec((CTX, 128), col(1)),
        pl.BlockSpec((CTX, 128), col(2)),
        pl.BlockSpec((1, NA_KH, 128, NA_WIN), lambda p, b: (p, 0, 0, 0)),
    ]
    args = [qkv_x, qkv_x, qkv_x, qkv_c, qkv_c, bias]
    out_specs = [pl.BlockSpec((SEQ, 128), lambda p, b: (b, p))]
    out_shape = [jax.ShapeDtypeStruct((BATCH * SEQ, D), BF16)]
    if need_ctx:
        in_specs.append(pl.BlockSpec((CTX, 128), col(0)))
        args.append(qkv_c)
        out_specs.append(pl.BlockSpec((CTX, 128), lambda p, b: (b, p)))
        out_shape.append(jax.ShapeDtypeStruct((BATCH * CTX, D), BF16))
    out = pl.pallas_call(
        functools.partial(_na_attn_kernel, need_ctx=need_ctx),
        grid=(NA_PAIRS, BATCH),
        in_specs=in_specs,
        out_specs=out_specs,
        out_shape=out_shape,
        scratch_shapes=[
            pltpu.VMEM((4, 2 * GRID_W, NA_WIN + CTX), F32),
            pltpu.VMEM((4, 2 * GRID_W, NA_WIN + CTX), BF16),
            pltpu.VMEM((4, 2 * GRID_W, 128), F32),
        ],
        compiler_params=_params("parallel", "parallel"),
        name="na_attention",
    )(*args)
    return (out[0], out[1]) if need_ctx else (out[0], None)


def na_bias_table(rpb):
    c = np.arange(GRID_W)[:, None]
    kc = np.arange(GRID_W)[None, :]
    cs = np.clip(c - NA_KW // 2, 0, GRID_W - NA_KW)
    valid = (kc >= cs) & (kc < cs + NA_KW)
    lpad = GRID_W - NA_KW
    p = jnp.pad(rpb, ((0, 0), (0, 0), (lpad, 128 - lpad - (2 * NA_KW - 1))))
    toep = jnp.tile(p, (1, 1, GRID_W))[:, :, :GRID_W * 127].reshape(NA_HEADS, 2 * NA_KH - 1, GRID_W, 127)
    toep = jnp.where(valid, toep[..., GRID_W - 1:] * LOG2E, NEG)
    t = jnp.stack([toep[:, NA_KH - 1 - var: 2 * NA_KH - 1 - var] for var in range(NA_KH)], axis=1)
    t = t.transpose(0, 1, 3, 2, 4).reshape(NA_PAIRS, 2, NA_KH, GRID_W, NA_WIN)
    return t.transpose(0, 2, 1, 3, 4).reshape(NA_PAIRS, NA_KH, 2 * GRID_W, NA_WIN)


def _proj_res_kernel(a_ref, w_ref, b_ref, gate_ref, res_ref, o_ref):
    y = _dot(a_ref[...], w_ref[...]) + b_ref[...]
    o_ref[...] = res_ref[...] + gate_ref[0] * y


def proj_residual(a, w, b, gate, res, *, tm, row_of):
    m, k = a.shape
    return pl.pallas_call(
        _proj_res_kernel,
        grid=(m // tm,),
        in_specs=[
            pl.BlockSpec((tm, k), lambda i: (i, 0)),
            pl.BlockSpec((k, D), lambda i: (0, 0)),
            _vec_spec(D), _mod_spec(row_of),
            pl.BlockSpec((tm, D), lambda i: (i, 0)),
        ],
        out_specs=pl.BlockSpec((tm, D), lambda i: (i, 0)),
        out_shape=jax.ShapeDtypeStruct((m, D), F32),
        compiler_params=_params("parallel"),
        name="proj_residual",
    )(a, w, b, gate, res)


def _ffn_kernel(xp_ref, x_ref, xn_ref, sh_ref, sc_ref, gate_ref, gain_ref, wu_ref, wv_ref, cw_ref, cb_ref,
                wd_ref, o_ref, h_scr, acc_scr, *, tm, tiles_per_seq):
    i = pl.program_id(0)
    j = pl.program_id(1)
    hl = 16

    @pl.when(j == 0)
    def _():
        gain, shift, scale = gain_ref[...], sh_ref[0], sc_ref[0]
        t = i % tiles_per_seq
        hp = _mod_norm(xp_ref[...], gain, shift, scale)
        hn = _mod_norm(xn_ref[...], gain, shift, scale)
        h_scr[0:hl] = jnp.where(t == 0, 0.0, hp).astype(BF16)
        h_scr[hl:hl + tm] = _mod_norm(x_ref[...], gain, shift, scale).astype(BF16)
        h_scr[hl + tm:] = jnp.where(t == tiles_per_seq - 1, 0.0, hn).astype(BF16)

    u = _dot(h_scr[...], wu_ref[...])
    v = _dot(h_scr[hl:hl + tm], wv_ref[...])
    cw = cw_ref[0]
    uc = (cw[0:1] * u[hl - 1:hl - 1 + tm] + cw[1:2] * u[hl:hl + tm] + cw[2:3] * u[hl + 1:hl + 1 + tm]
          + cb_ref[0])
    g = (jax.nn.gelu(uc, approximate=True) * v).astype(BF16)
    part = _dot(g, wd_ref[...])

    @pl.when(j == 0)
    def _():
        acc_scr[...] = part

    @pl.when(j > 0)
    def _():
        acc_scr[...] += part

    @pl.when(j == FFN_CHUNKS - 1)
    def _():
        o_ref[...] = x_ref[...] + gate_ref[0] * acc_scr[...]


def conv_ffn(x, sh, sc, gate, gain, wu, wv, cw, cb, wd, *, tm, seq, row_of):
    m = x.shape[0]
    hb = tm // 16
    last_hb = m // 16 - 1
    return pl.pallas_call(
        functools.partial(_ffn_kernel, tm=tm, tiles_per_seq=seq // tm),
        grid=(m // tm, FFN_CHUNKS),
        in_specs=[
            pl.BlockSpec((16, D), lambda i, j: (jnp.maximum(i * hb - 1, 0), 0)),
            pl.BlockSpec((tm, D), lambda i, j: (i, 0)),
            pl.BlockSpec((16, D), lambda i, j: (jnp.minimum((i + 1) * hb, last_hb), 0)),
            _mod_spec(row_of), _mod_spec(row_of), _mod_spec(row_of), _vec_spec(D),
            pl.BlockSpec((D, FFN_TH), lambda i, j: (0, j)),
            pl.BlockSpec((D, FFN_TH), lambda i, j: (0, j)),
            pl.BlockSpec((1, 3, FFN_TH), lambda i, j: (j, 0, 0)),
            pl.BlockSpec((1, 1, FFN_TH), lambda i, j: (j, 0, 0)),
            pl.BlockSpec((FFN_TH, D), lambda i, j: (j, 0)),
        ],
        out_specs=pl.BlockSpec((tm, D), lambda i, j: (i, 0)),
        out_shape=jax.ShapeDtypeStruct((m, D), F32),
        scratch_shapes=[pltpu.VMEM((tm + 32, D), BF16), pltpu.VMEM((tm, D), F32)],
        compiler_params=_params("parallel", "arbitrary"),
        name="conv_ffn",
    )(x, x, x, sh, sc, gate, gain, wu, wv, cw, cb, wd)


def _ret_proj_kernel(x_ref, sh_ref, sc_ref, gain_ref, w_ref, cos_ref, sin_ref, o_ref, h_scr, *, rope):
    j = pl.program_id(1)

    @pl.when(j == 0)
    def _():
        h_scr[...] = _mod_norm(x_ref[...], gain_ref[...], sh_ref[0], sc_ref[0]).astype(BF16)

    y = _dot(h_scr[...], w_ref[...])

    @pl.when(j == 0)
    def _():
        k_scale = RET_DK ** -0.5
        for hh in range(2 * RET_HEADS):
            s = k_scale if hh >= RET_HEADS else 1.0
            x1 = y[:, hh * RET_DK: hh * RET_DK + 128]
            x2 = y[:, hh * RET_DK + 128: (hh + 1) * RET_DK]
            if rope:
                cos, sin = cos_ref[...], sin_ref[...]
                x1, x2 = x1 * cos - x2 * sin, x1 * sin + x2 * cos
            o_ref[:, hh * RET_DK: hh * RET_DK + 128] = (x1 * s).astype(o_ref.dtype)
            o_ref[:, hh * RET_DK + 128: (hh + 1) * RET_DK] = (x2 * s).astype(o_ref.dtype)

    @pl.when(j == 1)
    def _():
        o_ref[...] = y.astype(o_ref.dtype)

    @pl.when(j == 2)
    def _():
        o_ref[...] = (y * jax.nn.sigmoid(y)).astype(o_ref.dtype)


def ret_proj(x, sh, sc, gain, w, cos, sin, *, tm, seq, row_of, rope):
    m = x.shape[0]
    tn = 2 * D
    tps = seq // tm
    return pl.pallas_call(
        functools.partial(_ret_proj_kernel, rope=rope),
        grid=(m // tm, 3),
        in_specs=[
            pl.BlockSpec((tm, D), lambda i, j: (i, 0)),
            _mod_spec(row_of), _mod_spec(row_of), _vec_spec(D),
            pl.BlockSpec((D, tn), lambda i, j: (0, j)),
            pl.BlockSpec((tm, 128), lambda i, j: (i % tps, 0)),
            pl.BlockSpec((tm, 128), lambda i, j: (i % tps, 0)),
        ],
        out_specs=pl.BlockSpec((tm, tn), lambda i, j: (i, j)),
        out_shape=jax.ShapeDtypeStruct((m, 3 * tn), BF16),
        scratch_shapes=[pltpu.VMEM((tm, D), BF16)],
        compiler_params=_params("parallel", "arbitrary"),
        name="ret_proj",
    )(x, sh, sc, gain, w, cos, sin)


def _log_sigmoid(x):
    return jnp.minimum(x, 0.0) - jnp.log1p(jnp.exp(-jnp.abs(x)))


def _ret_scan_kernel(qx_ref, kx_ref, vx_ref, qc_ref, kc_ref, vc_ref, dl_ref, ox_ref, oc_ref,
                     s_scr, ax_scr, ac_scr):
    C = RET_C
    pos = lax.broadcasted_iota(jnp.int32, (C, 128), 0).astype(F32)
    ii = lax.broadcasted_iota(jnp.int32, (C, C), 0).astype(F32)
    jj = lax.broadcasted_iota(jnp.int32, (C, C), 1).astype(F32)

    for direction in range(2):
        lg = jnp.broadcast_to(_log_sigmoid(dl_ref[direction, 0])[0:1, :], (C, 128))
        if direction == 0:
            diff = ii - jj
            p = pos
        else:
            diff = jj - ii
            p = (C - 1.0) - pos
        intra = jnp.where(diff >= 0, jnp.exp(jnp.maximum(diff, 0.0) * lg), 0.0)
        qd = jnp.exp((p + 1.0) * lg)
        kd = jnp.exp((C - 1.0 - p) * lg)
        cd = jnp.exp(C * lg)[0:1, :]
        qd_v = jnp.concatenate([qd] * (RET_DV // 128), axis=1)
        kd_k = jnp.concatenate([kd] * (RET_DK // 128), axis=1)
        cd_v = jnp.concatenate([cd] * (RET_DV // 128), axis=1)
        s_scr[...] = jnp.zeros_like(s_scr)

        def chunk(q_ref, k_ref, v_ref, a_ref, start):
            q = q_ref[pl.ds(start, C), :]
            k = k_ref[pl.ds(start, C), :]
            v = v_ref[pl.ds(start, C), :]
            scores = (_dot_nt(q, k) * intra).astype(BF16)
            s_old = s_scr[...]
            o = _dot(scores, v) + qd_v * _dot(q, s_old.astype(BF16))
            if direction == 0:
                a_ref[pl.ds(start, C), :] = o
            else:
                a_ref[pl.ds(start, C), :] += o
            kdec = (k.astype(F32) * kd_k).astype(BF16)
            s_scr[...] = s_old * cd_v + _dot_tn(kdec, v)

        n_c = CTX // C
        n_x = SEQ // C
        for t in range(n_c):
            cidx = t if direction == 0 else n_c - 1 - t
            chunk(qc_ref, kc_ref, vc_ref, ac_scr, cidx * C)

        def x_chunk(t, carry):
            cidx = t if direction == 0 else n_x - 1 - t
            chunk(qx_ref, kx_ref, vx_ref, ax_scr, pl.multiple_of(cidx * C, C))
            return carry

        lax.fori_loop(0, n_x, x_chunk, 0)

    def group_norm(o):
        mu = jnp.mean(o, axis=-1, keepdims=True)
        oc = o - mu
        return oc * lax.rsqrt(jnp.mean(oc * oc, axis=-1, keepdims=True) + EPS)

    oc_ref[...] = group_norm(ac_scr[...]).astype(oc_ref.dtype)
    rb = 512

    def gn_block(t, carry):
        r0 = pl.multiple_of(t * rb, rb)
        ox_ref[pl.ds(r0, rb), :] = group_norm(ax_scr[pl.ds(r0, rb), :]).astype(ox_ref.dtype)
        return carry

    lax.fori_loop(0, SEQ // rb, gn_block, 0)


def ret_scan(px, pc, dl_tiles):
    kb = RET_HEADS
    vb = 2 * RET_HEADS * RET_DK // RET_DV
    return pl.pallas_call(
        _ret_scan_kernel,
        grid=(BATCH, RET_HEADS),
        in_specs=[
            pl.BlockSpec((SEQ, RET_DK), lambda b, h: (b, h)),
            pl.BlockSpec((SEQ, RET_DK), lambda b, h: (b, kb + h)),
            pl.BlockSpec((SEQ, RET_DV), lambda b, h: (b, vb + h)),
            pl.BlockSpec((CTX, RET_DK), lambda b, h: (b, h)),
            pl.BlockSpec((CTX, RET_DK), lambda b, h: (b, kb + h)),
            pl.BlockSpec((CTX, RET_DV), lambda b, h: (b, vb + h)),
            pl.BlockSpec((2, 1, 8, 128), lambda b, h: (0, h, 0, 0)),
        ],
        out_specs=[
            pl.BlockSpec((SEQ, RET_DV), lambda b, h: (b, h)),
            pl.BlockSpec((CTX, RET_DV), lambda b, h: (b, h)),
        ],
        out_shape=[
            jax.ShapeDtypeStruct((BATCH * SEQ, RET_HEADS * RET_DV), BF16),
            jax.ShapeDtypeStruct((BATCH * CTX, RET_HEADS * RET_DV), BF16),
        ],
        scratch_shapes=[
            pltpu.VMEM((RET_DK, RET_DV), F32),
            pltpu.VMEM((SEQ, RET_DV), F32),
            pltpu.VMEM((CTX, RET_DV), F32),
        ],
        compiler_params=_params("parallel", "parallel"),
        name="ret_scan",
    )(px, px, px, pc, pc, pc, dl_tiles)


def _ret_out_kernel(o_ref, g_ref, gn_ref, w_ref, gate_ref, res_ref, out_ref):
    a = (o_ref[...].astype(F32) * gn_ref[...] * g_ref[...].astype(F32)).astype(BF16)
    out_ref[...] = res_ref[...] + gate_ref[0] * _dot(a, w_ref[...])


def ret_out(o, proj, gn, w, gate, res, *, tm, row_of):
    m = o.shape[0]
    dv = RET_HEADS * RET_DV
    return pl.pallas_call(
        _ret_out_kernel,
        grid=(m // tm,),
        in_specs=[
            pl.BlockSpec((tm, dv), lambda i: (i, 0)),
            pl.BlockSpec((tm, dv), lambda i: (i, 2)),
            _vec_spec(dv),
            pl.BlockSpec((dv, D), lambda i: (0, 0)),
            _mod_spec(row_of),
            pl.BlockSpec((tm, D), lambda i: (i, 0)),
        ],
        out_specs=pl.BlockSpec((tm, D), lambda i: (i, 0)),
        out_shape=jax.ShapeDtypeStruct((m, D), F32),
        compiler_params=_params("parallel"),
        name="ret_out",
    )(o, proj, gn, w, gate, res)


def _pw1_glu_kernel(x_ref, sh_ref, sc_ref, gain_ref, w_ref, b_ref, o_ref):
    h = _mod_norm(x_ref[...], gain_ref[...], sh_ref[0], sc_ref[0]).astype(BF16)
    y = _dot(h, w_ref[...]) + b_ref[...]
    o_ref[...] = y[:, :D] * jax.nn.sigmoid(y[:, D:])


def pw1_glu(x, sh, sc, gain, w, b, *, tm, row_of):
    m = x.shape[0]
    return pl.pallas_call(
        _pw1_glu_kernel,
        grid=(m // tm,),
        in_specs=[
            pl.BlockSpec((tm, D), lambda i: (i, 0)),
            _mod_spec(row_of), _mod_spec(row_of), _vec_spec(D),
            pl.BlockSpec((D, 2 * D), lambda i: (0, 0)),
            _vec_spec(2 * D),
        ],
        out_specs=pl.BlockSpec((tm, D), lambda i: (i, 0)),
        out_shape=jax.ShapeDtypeStruct((m, D), F32),
        compiler_params=_params("parallel"),
        name="pw1_glu",
    )(x, sh, sc, gain, w, b)


def _dwconv_kernel(up_ref, u_ref, un_ref, cw_ref, cb_ref, lg_ref, lb_ref, w_ref, b_ref, gate_ref, res_ref,
                   o_ref, ue_scr, *, tm, tiles_per_seq):
    t = pl.program_id(0) % tiles_per_seq
    hl = CONV_HALO
    ue_scr[0:hl] = jnp.where(t == 0, 0.0, up_ref[...])
    ue_scr[hl:hl + tm] = u_ref[...]
    ue_scr[hl + tm:] = jnp.where(t == tiles_per_seq - 1, 0.0, un_ref[...])
    base = hl - CONV_W // 2
    span = tm + 8 * ((base + CONV_W - 1) // 8)
    acc = jnp.zeros((tm, D), F32)
    for b in range(8):
        ub = ue_scr[pl.ds(b, span), :]
        for a in range((base + CONV_W - 1) // 8 + 1):
            k = 8 * a + b - base
            if 0 <= k < CONV_W:
                acc = acc + cw_ref[k:k + 1, :] * ub[8 * a:8 * a + tm]
    y = acc + cb_ref[...]
    mu = jnp.mean(y, axis=-1, keepdims=True)
    yc = y - mu
    var = jnp.mean(yc * yc, axis=-1, keepdims=True)
    z = yc * lax.rsqrt(var + EPS) * lg_ref[...] + lb_ref[...]
    z = (z * jax.nn.sigmoid(z)).astype(BF16)
    o_ref[...] = res_ref[...] + gate_ref[0] * (_dot(z, w_ref[...]) + b_ref[...])


def dwconv_out(u, cw, cb, lg, lb, w, b, gate, res, *, tm, seq, row_of):
    m = u.shape[0]
    hb = tm // CONV_HALO
    last_hb = m // CONV_HALO - 1
    return pl.pallas_call(
        functools.partial(_dwconv_kernel, tm=tm, tiles_per_seq=seq // tm),
        grid=(m // tm,),
        in_specs=[
            pl.BlockSpec((CONV_HALO, D), lambda i: (jnp.maximum(i * hb - 1, 0), 0)),
            pl.BlockSpec((tm, D), lambda i: (i, 0)),
            pl.BlockSpec((CONV_HALO, D), lambda i: (jnp.minimum((i + 1) * hb, last_hb), 0)),
            pl.BlockSpec((CONV_W, D), lambda i: (0, 0)),
            _vec_spec(D), _vec_spec(D), _vec_spec(D),
            pl.BlockSpec((D, D), lambda i: (0, 0)),
            _vec_spec(D), _mod_spec(row_of),
            pl.BlockSpec((tm, D), lambda i: (i, 0)),
        ],
        out_specs=pl.BlockSpec((tm, D), lambda i: (i, 0)),
        out_shape=jax.ShapeDtypeStruct((m, D), F32),
        scratch_shapes=[pltpu.VMEM((tm + 2 * CONV_HALO, D), F32)],
        compiler_params=_params("parallel"),
        name="dwconv_out",
    )(u, u, u, cw, cb, lg, lb, w, b, gate, res)


def _rope_tables():
    t = np.arange(SEQ)
    row = (t // GRID_W).astype(np.float32)
    col = (t % GRID_W).astype(np.float32)
    n_freq = RET_DK // 4
    inv_freq = jnp.power(ROPE_BASE, -jnp.arange(n_freq, dtype=F32) / n_freq)
    ang = jnp.concatenate([row[:, None] * inv_freq, col[:, None] * inv_freq], axis=-1)
    return jnp.cos(ang), jnp.sin(ang)


def kernel(x, c, ctx, c_ctx, ada_w, ada_b, norm_mix, norm_ffn, na_w_qkv, na_w_o, na_q_gain, na_k_gain, na_rpb, ret_w_q, ret_w_k, ret_w_v, ret_w_g, ret_w_o, ret_gn_gain, ret_decay_logit, cv_w_pw1, cv_b_pw1, cv_w_dw, cv_b_dw, cv_ln_g, cv_ln_b, cv_w_pw2, cv_b_pw2, ffn_w_up, ffn_w_dw, ffn_b_dw, ffn_w_down):
    xs = x.reshape(BATCH * SEQ, D)
    cs = ctx.reshape(BATCH * CTX, D)
    tm_x, tm_c = 512, CTX
    tiles_per_batch = SEQ // tm_x

    def x_row(i):
        return i // tiles_per_batch

    def c_row(i):
        return CTX_ROW

    c_rows = jnp.zeros((MOD_ROWS, D), F32).at[:BATCH].set(c).at[CTX_ROW].set(c_ctx)
    mods = ada_table(c_rows, ada_w, ada_b).reshape(DEPTH, MOD_ROWS, 6, 1, D)
    zero_bias = jnp.zeros((1, D), F32)
    cos, sin = _rope_tables()

    for i in range(DEPTH):
        need_ctx = i < DEPTH - 1
        mod = [mods[i, :, k] for k in range(6)]
        g_mix = norm_mix[i].reshape(1, D)
        g_ffn = norm_ffn[i].reshape(1, D)
        kind, j = i % 3, i // 3
        if kind == 0:
            w = na_w_qkv[j].astype(BF16)
            qg = (jnp.tile(na_q_gain[j], NA_HEADS) * (NA_SCALE * LOG2E)).reshape(1, D)
            kg = jnp.tile(na_k_gain[j], NA_HEADS).reshape(1, D)
            qkv_x = na_qkv(xs, mod[0], mod[1], g_mix, w, qg, kg, tm=tm_x, row_of=x_row)
            qkv_c = na_qkv(cs, mod[0], mod[1], g_mix, w, qg, kg, tm=tm_c, row_of=c_row)
            ox, oc = na_attention(qkv_x, qkv_c, na_bias_table(na_rpb[j]), need_ctx=need_ctx)
            w_o = na_w_o[j].astype(BF16)
            xs = proj_residual(ox, w_o, zero_bias, mod[2], xs, tm=tm_x, row_of=x_row)
            if need_ctx:
                cs = proj_residual(oc, w_o, zero_bias, mod[2], cs, tm=tm_c, row_of=c_row)
        elif kind == 1:
            w = jnp.concatenate([ret_w_q[j], ret_w_k[j], ret_w_v[j], ret_w_g[j]], axis=1).astype(BF16)
            px = ret_proj(xs, mod[0], mod[1], g_mix, w, cos, sin, tm=tm_x, seq=SEQ, row_of=x_row, rope=True)
            pc = ret_proj(cs, mod[0], mod[1], g_mix, w, cos[:CTX], sin[:CTX], tm=tm_c, seq=CTX, row_of=c_row,
                          rope=False)
            dl = jnp.broadcast_to(ret_decay_logit[j][:, :, None, None], (2, RET_HEADS, 8, 128))
            ox, oc = ret_scan(px, pc, dl)
            w_o = ret_w_o[j].astype(BF16)
            gn = ret_gn_gain[j].reshape(1, RET_HEADS * RET_DV)
            xs = ret_out(ox, px, gn, w_o, mod[2], xs, tm=tm_x, row_of=x_row)
            if need_ctx:
                cs = ret_out(oc, pc, gn, w_o, mod[2], cs, tm=tm_c, row_of=c_row)
        else:
            w1 = cv_w_pw1[j].astype(BF16)
            b1 = cv_b_pw1[j].reshape(1, 2 * D)
            w2 = cv_w_pw2[j].astype(BF16)
            conv_args = (cv_w_dw[j], cv_b_dw[j].reshape(1, D), cv_ln_g[j].reshape(1, D), cv_ln_b[j].reshape(1, D),
                         w2, cv_b_pw2[j].reshape(1, D))
            ux = pw1_glu(xs, mod[0], mod[1], g_mix, w1, b1, tm=tm_x, row_of=x_row)
            xs = dwconv_out(ux, *conv_args, mod[2], xs, tm=256, seq=SEQ, row_of=lambda t: t // (SEQ // 256))
            if need_ctx:
                uc = pw1_glu(cs, mod[0], mod[1], g_mix, w1, b1, tm=tm_c, row_of=c_row)
                cs = dwconv_out(uc, *conv_args, mod[2], cs, tm=tm_c, seq=CTX, row_of=c_row)

        w_up = ffn_w_up[i].astype(BF16)
        wu, wv = w_up[:, :FFN_H], w_up[:, FFN_H:]
        cw = ffn_w_dw[i].reshape(3, FFN_CHUNKS, FFN_TH).transpose(1, 0, 2)
        cb = ffn_b_dw[i].reshape(FFN_CHUNKS, 1, FFN_TH)
        wd = ffn_w_down[i].astype(BF16)
        xs = conv_ffn(xs, mod[3], mod[4], mod[5], g_ffn, wu, wv, cw, cb, wd, tm=tm_x, seq=SEQ, row_of=x_row)
        if need_ctx:
            cs = conv_ffn(cs, mod[3], mod[4], mod[5], g_ffn, wu, wv, cw, cb, wd, tm=tm_c, seq=CTX, row_of=c_row)

    return xs.reshape(BATCH, SEQ, D)
```

```python
import functools

import numpy as np
import jax
import jax.numpy as jnp
from jax import lax
from jax.experimental import pallas as pl
from jax.experimental.pallas import tpu as pltpu

D = 1024
BATCH = 4
SEQ = 4096
DEPTH = 4
CTX = 256
GRID_W = 64
ROWS = SEQ // GRID_W
EPS = 1e-6

NA_HEADS = 16
NA_DH = 64
NA_KH = 8
NA_KW = 16
NA_SCALE = NA_DH ** -0.5
NA_PAIRS = NA_HEADS // 2
NA_WIN = NA_KH * GRID_W
NA_GROUP = 2

RET_HEADS = 4
RET_DK = 256
RET_DV = 512
RET_C = 256
assert CTX == RET_C and (SEQ // RET_C) % 2 == 0
ROPE_BASE = 10000.0

CONV_W = 31
CONV_HALO = 16
CONV_RB = 64
FFN_H = 2816
FFN_SPLITS = (0, 1536, FFN_H)

MOD_ROWS = 8
CTX_ROW = BATCH
NEG = -1e30
LOG2E = 1.4426950408889634

F32 = jnp.float32
BF16 = jnp.bfloat16

VMEM_LIMIT = 52 * 1024 * 1024


def _params(*sem):
    return pltpu.CompilerParams(dimension_semantics=sem, vmem_limit_bytes=VMEM_LIMIT)


def _mod_norm(x, gain, shift, scale):
    ms = jnp.mean(x * x, axis=-1, keepdims=True)
    return x * lax.rsqrt(ms + EPS) * gain * (1.0 + scale) + shift


def _dot(a, b):
    return jnp.dot(a, b, preferred_element_type=F32)


def _dot_nt(a, b):
    return lax.dot_general(a, b, (((1,), (1,)), ((), ())), preferred_element_type=F32)


def _dot_tn(a, b):
    return lax.dot_general(a, b, (((0,), (0,)), ((), ())), preferred_element_type=F32)


def _ada_kernel(c_ref, w_ref, b_ref, o_ref):
    c = c_ref[...]
    s = (c * jax.nn.sigmoid(c)).astype(BF16)
    o_ref[0] = _dot(s, w_ref[0].astype(BF16)) + b_ref[0]


def ada_table(c_rows, ada_w, ada_b):
    tn = 1536
    return pl.pallas_call(
        _ada_kernel,
        grid=(DEPTH, 6 * D // tn),
        in_specs=[
            pl.BlockSpec((MOD_ROWS, D), lambda l, j: (0, 0)),
            pl.BlockSpec((1, D, tn), lambda l, j: (l, 0, j)),
            pl.BlockSpec((1, 1, tn), lambda l, j: (l, 0, j)),
        ],
        out_specs=pl.BlockSpec((1, MOD_ROWS, tn), lambda l, j: (l, 0, j)),
        out_shape=jax.ShapeDtypeStruct((DEPTH, MOD_ROWS, 6 * D), F32),
        compiler_params=_params("parallel", "parallel"),
        name="ada_table",
    )(c_rows, ada_w, ada_b.reshape(DEPTH, 1, 6 * D))


def _mod_spec(row_of):
    return pl.BlockSpec((1, 1, D), lambda i, *_: (row_of(i), 0, 0))


def _vec_spec(n):
    return pl.BlockSpec((1, n), lambda i, *_: (0, 0))


def _na_qkv_kernel(x_ref, sh_ref, sc_ref, gain_ref, w_ref, qg_ref, kg_ref, o_ref):
    h = _mod_norm(x_ref[...], gain_ref[...], sh_ref[0], sc_ref[0]).astype(BF16)
    r = lax.broadcasted_iota(jnp.int32, (128, 128), 0) // NA_DH
    c = lax.broadcasted_iota(jnp.int32, (128, 128), 1) // NA_DH
    seg = jnp.where(r == c, 1.0, 0.0).astype(BF16)
    for sec, g_ref in ((0, qg_ref), (1, kg_ref)):
        y = _dot(h, w_ref[:, sec * D:(sec + 1) * D])
        for g in range(D // 128):
            yg = y[:, g * 128:(g + 1) * 128]
            ss = _dot((yg * yg).astype(BF16), seg)
            yn = yg * lax.rsqrt(ss * (1.0 / NA_DH) + EPS) * g_ref[:, g * 128:(g + 1) * 128]
            o_ref[:, sec * D + g * 128: sec * D + (g + 1) * 128] = yn.astype(o_ref.dtype)
    o_ref[:, 2 * D:] = _dot(h, w_ref[:, 2 * D:]).astype(o_ref.dtype)


def na_qkv(x, sh, sc, gain, w, qg, kg, *, tm, row_of):
    m = x.shape[0]
    return pl.pallas_call(
        _na_qkv_kernel,
        grid=(m // tm,),
        in_specs=[
            pl.BlockSpec((tm, D), lambda i: (i, 0)),
            _mod_spec(row_of), _mod_spec(row_of), _vec_spec(D),
            pl.BlockSpec((D, 3 * D), lambda i: (0, 0)),
            _vec_spec(D), _vec_spec(D),
        ],
        out_specs=pl.BlockSpec((tm, 3 * D), lambda i: (i, 0)),
        out_shape=jax.ShapeDtypeStruct((m, 3 * D), BF16),
        compiler_params=_params("parallel"),
        name="na_qkv",
    )(x, sh, sc, gain, w, qg, kg)


def _split_heads(q, in_a):
    zero = jnp.zeros_like(q)
    return jnp.concatenate([jnp.where(in_a, q, zero), jnp.where(in_a, zero, q)], axis=0)


def _na_attn_kernel(q_ref, k_ref, v_ref, kc_ref, vc_ref, bias_ref, *rest, need_ctx):
    if need_ctx:
        qc_ref, o_ref, oc_ref, s_scr, p_scr, l_scr = rest
    else:
        o_ref, s_scr, p_scr, l_scr = rest
    in_a = lax.broadcasted_iota(jnp.int32, (1, 128), 1) < NA_DH
    kc = kc_ref[...]
    vc = vc_ref[...]

    def window(r):
        if isinstance(r, int):
            rs = min(max(r - NA_KH // 2, 0), ROWS - NA_KH)
            return r * GRID_W, rs * GRID_W, r - rs
        rs = jnp.clip(r - NA_KH // 2, 0, ROWS - NA_KH)
        return pl.multiple_of(r * GRID_W, GRID_W), pl.multiple_of(rs * GRID_W, GRID_W), r - rs

    def scores(r, slot):
        q0, k0, var = window(r)
        q2 = _split_heads(q_ref[pl.ds(q0, GRID_W), :], in_a)
        s_scr[slot, :, :NA_WIN] = _dot_nt(q2, k_ref[pl.ds(k0, NA_WIN), :]) + bias_ref[0, var]
        s_scr[slot, :, NA_WIN:] = _dot_nt(q2, kc)

    def softmax(slot):
        s = s_scr[slot]
        p = jnp.exp2(s - jnp.max(s, axis=-1, keepdims=True))
        l_scr[slot] = jnp.broadcast_to(jnp.sum(p, axis=-1, keepdims=True), (2 * GRID_W, 128))
        p_scr[slot] = p.astype(BF16)

    def values(r, slot):
        q0, k0, _ = window(r)
        p = p_scr[slot]
        o2 = (_dot(p[:, :NA_WIN], v_ref[pl.ds(k0, NA_WIN), :]) + _dot(p[:, NA_WIN:], vc)) / l_scr[slot]
        o_ref[pl.ds(q0, GRID_W), :] = jnp.where(in_a, o2[:GRID_W], o2[GRID_W:]).astype(o_ref.dtype)

    def stage_values(t):
        for u in range(NA_GROUP):
            values(NA_GROUP * t + u, NA_GROUP * (t % 2) + u)

    def stage_softmax(t):
        for u in range(NA_GROUP):
            softmax(NA_GROUP * (t % 2) + u)

    def stage_scores(t):
        for u in range(NA_GROUP):
            scores(NA_GROUP * t + u, NA_GROUP * (t % 2) + u)

    n_groups = ROWS // NA_GROUP
    stage_scores(0)
    stage_softmax(0)
    stage_scores(1)

    def step(t, carry):
        stage_values(t - 2)
        stage_softmax(t - 1)
        stage_scores(t)
        return carry

    lax.fori_loop(2, n_groups, step, 0)
    stage_values(n_groups - 2)
    stage_softmax(n_groups - 1)
    stage_values(n_groups - 1)

    if need_ctx:
        q2 = _split_heads(qc_ref[...], in_a)
        s = _dot_nt(q2, kc)
        p = jnp.exp2(s - jnp.max(s, axis=-1, keepdims=True))
        o2 = _dot(p.astype(BF16), vc) / jnp.sum(p, axis=-1, keepdims=True)
        oc_ref[...] = jnp.where(in_a, o2[:CTX], o2[CTX:]).astype(oc_ref.dtype)


def na_attention(qkv_x, qkv_c, bias, *, need_ctx):
    def col(sec):
        return lambda p, b: (b, sec * NA_PAIRS + p)
    in_specs = [
        pl.BlockSpec((SEQ, 128), col(0)),
        pl.BlockSpec((SEQ, 128), col(1)),
        pl.BlockSpec((SEQ, 128), col(2)),
        pl.BlockSpec((CTX, 128), col(1)),
        pl.BlockSpec((CTX, 128), col(2)),
        pl.BlockSpec((1, NA_KH, 128, NA_WIN), lambda p, b: (p, 0, 0, 0)),
    ]
    args = [qkv_x, qkv_x, qkv_x, qkv_c, qkv_c, bias]
    out_specs = [pl.BlockSpec((SEQ, 128), lambda p, b: (b, p))]
    out_shape = [jax.ShapeDtypeStruct((BATCH * SEQ, D), BF16)]
    if need_ctx:
        in_specs.append(pl.BlockSpec((CTX, 128), col(0)))
        args.append(qkv_c)
        out_specs.append(pl.BlockSpec((CTX, 128), lambda p, b: (b, p)))
        out_shape.append(jax.ShapeDtypeStruct((BATCH * CTX, D), BF16))
    out = pl.pallas_call(
        functools.partial(_na_attn_kernel, need_ctx=need_ctx),
        grid=(NA_PAIRS, BATCH),
        in_specs=in_specs,
        out_specs=out_specs,
        out_shape=out_shape,
        scratch_shapes=[
            pltpu.VMEM((2 * NA_GROUP, 2 * GRID_W, NA_WIN + CTX), F32),
            pltpu.VMEM((2 * NA_GROUP, 2 * GRID_W, NA_WIN + CTX), BF16),
            pltpu.VMEM((2 * NA_GROUP, 2 * GRID_W, 128), F32),
        ],
        compiler_params=_params("parallel", "parallel"),
        name="na_attention",
    )(*args)
    return (out[0], out[1]) if need_ctx else (out[0], None)


def na_bias_table(rpb):
    c = np.arange(GRID_W)[:, None]
    kc = np.arange(GRID_W)[None, :]
    cs = np.clip(c - NA_KW // 2, 0, GRID_W - NA_KW)
    valid = (kc >= cs) & (kc < cs + NA_KW)
    lpad = GRID_W - NA_KW
    p = jnp.pad(rpb, ((0, 0), (0, 0), (lpad, 128 - lpad - (2 * NA_KW - 1))))
    toep = jnp.tile(p, (1, 1, GRID_W))[:, :, :GRID_W * 127].reshape(NA_HEADS, 2 * NA_KH - 1, GRID_W, 127)
    toep = jnp.where(valid, toep[..., GRID_W - 1:] * LOG2E, NEG)
    t = jnp.stack([toep[:, NA_KH - 1 - var: 2 * NA_KH - 1 - var] for var in range(NA_KH)], axis=1)
    t = t.transpose(0, 1, 3, 2, 4).reshape(NA_PAIRS, 2, NA_KH, GRID_W, NA_WIN)
    return t.transpose(0, 2, 1, 3, 4).reshape(NA_PAIRS, NA_KH, 2 * GRID_W, NA_WIN)


def _proj_res_kernel(a_ref, w_ref, b_ref, gate_ref, res_ref, o_ref):
    y = _dot(a_ref[...], w_ref[...]) + b_ref[...]
    o_ref[...] = res_ref[...] + gate_ref[0] * y


def proj_residual(a, w, b, gate, res, *, tm, row_of):
    m, k = a.shape
    return pl.pallas_call(
        _proj_res_kernel,
        grid=(m // tm,),
        in_specs=[
            pl.BlockSpec((tm, k), lambda i: (i, 0)),
            pl.BlockSpec((k, D), lambda i: (0, 0)),
            _vec_spec(D), _mod_spec(row_of),
            pl.BlockSpec((tm, D), lambda i: (i, 0)),
        ],
        out_specs=pl.BlockSpec((tm, D), lambda i: (i, 0)),
        out_shape=jax.ShapeDtypeStruct((m, D), F32),
        compiler_params=_params("parallel"),
        name="proj_residual",
    )(a, w, b, gate, res)


def _ffn_kernel(xp_ref, x_ref, xn_ref, sh_ref, sc_ref, gate_ref, gain_ref, wup_ref, cw_ref, cb_ref, wd_ref,
                o_ref, h_scr, *, tm, tiles_per_seq):
    hl = 16
    gain, shift, scale = gain_ref[...], sh_ref[0], sc_ref[0]
    t = pl.program_id(0) % tiles_per_seq
    hp = _mod_norm(xp_ref[...], gain, shift, scale)
    hn = _mod_norm(xn_ref[...], gain, shift, scale)
    h_scr[0:hl] = jnp.where(t == 0, 0.0, hp).astype(BF16)
    h_scr[hl:hl + tm] = _mod_norm(x_ref[...], gain, shift, scale).astype(BF16)
    h_scr[hl + tm:] = jnp.where(t == tiles_per_seq - 1, 0.0, hn).astype(BF16)

    acc = None
    for c0, c1 in zip(FFN_SPLITS[:-1], FFN_SPLITS[1:]):
        u = _dot(h_scr[...], wup_ref[:, c0:c1])
        v = _dot(h_scr[hl:hl + tm], wup_ref[:, FFN_H + c0:FFN_H + c1])
        uc = (cw_ref[0:1, c0:c1] * u[hl - 1:hl - 1 + tm] + cw_ref[1:2, c0:c1] * u[hl:hl + tm]
              + cw_ref[2:3, c0:c1] * u[hl + 1:hl + 1 + tm] + cb_ref[:, c0:c1])
        g = (jax.nn.gelu(uc, approximate=True) * v).astype(BF16)
        part = _dot(g, wd_ref[c0:c1, :])
        acc = part if acc is None else acc + part
    o_ref[...] = x_ref[...] + gate_ref[0] * acc


def conv_ffn(x, sh, sc, gate, gain, w_up, cw, cb, wd, *, tm, seq, row_of):
    m = x.shape[0]
    hb = tm // 16
    last_hb = m // 16 - 1

    def resident(shape):
        return pl.BlockSpec(shape, lambda i: (0, 0), pipeline_mode=pl.Buffered(1))

    return pl.pallas_call(
        functools.partial(_ffn_kernel, tm=tm, tiles_per_seq=seq // tm),
        grid=(m // tm,),
        in_specs=[
            pl.BlockSpec((16, D), lambda i: (jnp.maximum(i * hb - 1, 0), 0)),
            pl.BlockSpec((tm, D), lambda i: (i, 0)),
            pl.BlockSpec((16, D), lambda i: (jnp.minimum((i + 1) * hb, last_hb), 0)),
            _mod_spec(row_of), _mod_spec(row_of), _mod_spec(row_of), _vec_spec(D),
            resident((D, 2 * FFN_H)), resident((3, FFN_H)), resident((1, FFN_H)), resident((FFN_H, D)),
        ],
        out_specs=pl.BlockSpec((tm, D), lambda i: (i, 0)),
        out_shape=jax.ShapeDtypeStruct((m, D), F32),
        scratch_shapes=[pltpu.VMEM((tm + 32, D), BF16)],
        compiler_params=_params("parallel"),
        name="conv_ffn",
    )(x, x, x, sh, sc, gate, gain, w_up, cw, cb, wd)


def _ret_proj_kernel(x_ref, sh_ref, sc_ref, gain_ref, w_ref, cos_ref, sin_ref, o_ref, *, rope):
    h = _mod_norm(x_ref[...], gain_ref[...], sh_ref[0], sc_ref[0]).astype(BF16)
    qk_width = 2 * RET_HEADS * RET_DK
    v_end = qk_width + RET_HEADS * RET_DV
    k_scale = RET_DK ** -0.5
    for hh in range(2 * RET_HEADS):
        y = _dot(h, w_ref[:, hh * RET_DK:(hh + 1) * RET_DK])
        s = k_scale if hh >= RET_HEADS else 1.0
        x1, x2 = y[:, :128], y[:, 128:]
        if rope:
            cos, sin = cos_ref[...], sin_ref[...]
            x1, x2 = x1 * cos - x2 * sin, x1 * sin + x2 * cos
        o_ref[:, hh * RET_DK: hh * RET_DK + 128] = (x1 * s).astype(o_ref.dtype)
        o_ref[:, hh * RET_DK + 128: (hh + 1) * RET_DK] = (x2 * s).astype(o_ref.dtype)
    o_ref[:, qk_width:v_end] = _dot(h, w_ref[:, qk_width:v_end]).astype(o_ref.dtype)
    g = _dot(h, w_ref[:, v_end:])
    o_ref[:, v_end:] = (g * jax.nn.sigmoid(g)).astype(o_ref.dtype)


def ret_proj(x, sh, sc, gain, w, cos, sin, *, tm, seq, row_of, rope):
    m = x.shape[0]
    n = w.shape[1]
    tps = seq // tm
    return pl.pallas_call(
        functools.partial(_ret_proj_kernel, rope=rope),
        grid=(m // tm,),
        in_specs=[
            pl.BlockSpec((tm, D), lambda i: (i, 0)),
            _mod_spec(row_of), _mod_spec(row_of), _vec_spec(D),
            pl.BlockSpec((D, n), lambda i: (0, 0), pipeline_mode=pl.Buffered(1)),
            pl.BlockSpec((tm, 128), lambda i: (i % tps, 0)),
            pl.BlockSpec((tm, 128), lambda i: (i % tps, 0)),
        ],
        out_specs=pl.BlockSpec((tm, n), lambda i: (i, 0)),
        out_shape=jax.ShapeDtypeStruct((m, n), BF16),
        compiler_params=_params("parallel"),
        name="ret_proj",
    )(x, sh, sc, gain, w, cos, sin)


def _log_sigmoid(x):
    return jnp.minimum(x, 0.0) - jnp.log1p(jnp.exp(-jnp.abs(x)))


def _ret_scan_kernel(qx_ref, kx_ref, vx_ref, qc_ref, kc_ref, vc_ref, dl_ref, ox_ref, oc_ref,
                     sf_scr, sb_scr, ax_scr, mask_scr):
    C = RET_C
    n = SEQ // C
    pos = lax.broadcasted_iota(jnp.int32, (C, 128), 0).astype(F32)
    ii = lax.broadcasted_iota(jnp.int32, (C, C), 0).astype(F32)
    jj = lax.broadcasted_iota(jnp.int32, (C, C), 1).astype(F32)

    def lanes(x, width):
        return jnp.concatenate([x] * (width // 128), axis=1)

    def decays(direction):
        lg = jnp.broadcast_to(_log_sigmoid(dl_ref[direction, 0])[0:1, :], (C, 128))
        diff = ii - jj if direction == 0 else jj - ii
        p = pos if direction == 0 else (C - 1.0) - pos
        intra = jnp.where(diff >= 0, jnp.exp(jnp.maximum(diff, 0.0) * lanes(lg, C)), 0.0)
        qd = jnp.exp((p + 1.0) * lg)
        kd = jnp.exp((C - 1.0 - p) * lg)
        cd = jnp.exp(C * lg)[0:1]
        return intra, qd, kd, cd

    intra_f, qd_f, kd_f, cd_f = decays(0)
    intra_b, qd_b, kd_b, cd_b = decays(1)
    mask_scr[...] = intra_f + intra_b

    def group_norm(o):
        mu = jnp.mean(o, axis=-1, keepdims=True)
        oc = o - mu
        return oc * lax.rsqrt(jnp.mean(oc * oc, axis=-1, keepdims=True) + EPS)

    def intra_term(q, k, v):
        return _dot((_dot_nt(q, k) * mask_scr[...]).astype(BF16), v)

    def cross_term(q, s_scr, qd):
        return lanes(qd, RET_DV) * _dot(q, s_scr[...].astype(BF16))

    def state_term(k, v, kd):
        return _dot_tn((k.astype(F32) * lanes(kd, RET_DK)).astype(BF16), v)

    def advance(s_scr, k, v, kd, cd):
        s_scr[...] = s_scr[...] * lanes(cd, RET_DV) + state_term(k, v, kd)

    q, k, v = qc_ref[...], kc_ref[...], vc_ref[...]
    oc_ref[...] = group_norm(intra_term(q, k, v)).astype(oc_ref.dtype)
    sf_scr[...] = state_term(k, v, kd_f)
    sb_scr[...] = state_term(k, v, kd_b)

    def visit(t, finish):
        rf = pl.multiple_of(t * C, C)
        rb = pl.multiple_of((n - 1 - t) * C, C)
        q, k, v = qx_ref[pl.ds(rf, C), :], kx_ref[pl.ds(rf, C), :], vx_ref[pl.ds(rf, C), :]
        o = intra_term(q, k, v) + cross_term(q, sf_scr, qd_f)
        if finish:
            ox_ref[pl.ds(rf, C), :] = group_norm(ax_scr[pl.ds(rf, C), :] + o).astype(ox_ref.dtype)
        else:
            ax_scr[pl.ds(rf, C), :] = o
        advance(sf_scr, k, v, kd_f, cd_f)
        q, k, v = qx_ref[pl.ds(rb, C), :], kx_ref[pl.ds(rb, C), :], vx_ref[pl.ds(rb, C), :]
        o = cross_term(q, sb_scr, qd_b)
        if finish:
            ox_ref[pl.ds(rb, C), :] = group_norm(ax_scr[pl.ds(rb, C), :] + o).astype(ox_ref.dtype)
        else:
            ax_scr[pl.ds(rb, C), :] = o
        advance(sb_scr, k, v, kd_b, cd_b)

    def first_visits(t, carry):
        visit(t, False)
        return carry

    def second_visits(t, carry):
        visit(t, True)
        return carry

    lax.fori_loop(0, n // 2, first_visits, 0)
    lax.fori_loop(n // 2, n, second_visits, 0)


def ret_scan(px, pc, dl_tiles):
    kb = RET_HEADS
    vb = 2 * RET_HEADS * RET_DK // RET_DV
    return pl.pallas_call(
        _ret_scan_kernel,
        grid=(BATCH, RET_HEADS),
        in_specs=[
            pl.BlockSpec((SEQ, RET_DK), lambda b, h: (b, h)),
            pl.BlockSpec((SEQ, RET_DK), lambda b, h: (b, kb + h)),
            pl.BlockSpec((SEQ, RET_DV), lambda b, h: (b, vb + h)),
            pl.BlockSpec((CTX, RET_DK), lambda b, h: (b, h)),
            pl.BlockSpec((CTX, RET_DK), lambda b, h: (b, kb + h)),
            pl.BlockSpec((CTX, RET_DV), lambda b, h: (b, vb + h)),
            pl.BlockSpec((2, 1, 8, 128), lambda b, h: (0, h, 0, 0)),
        ],
        out_specs=[
            pl.BlockSpec((SEQ, RET_DV), lambda b, h: (b, h)),
            pl.BlockSpec((CTX, RET_DV), lambda b, h: (b, h)),
        ],
        out_shape=[
            jax.ShapeDtypeStruct((BATCH * SEQ, RET_HEADS * RET_DV), BF16),
            jax.ShapeDtypeStruct((BATCH * CTX, RET_HEADS * RET_DV), BF16),
        ],
        scratch_shapes=[
            pltpu.VMEM((RET_DK, RET_DV), F32),
            pltpu.VMEM((RET_DK, RET_DV), F32),
            pltpu.VMEM((SEQ, RET_DV), F32),
            pltpu.VMEM((RET_C, RET_C), F32),
        ],
        compiler_params=_params("parallel", "parallel"),
        name="ret_scan",
    )(px, px, px, pc, pc, pc, dl_tiles)


def _ret_out_kernel(o_ref, g_ref, gn_ref, w_ref, gate_ref, res_ref, out_ref):
    a = (o_ref[...].astype(F32) * gn_ref[...] * g_ref[...].astype(F32)).astype(BF16)
    out_ref[...] = res_ref[...] + gate_ref[0] * _dot(a, w_ref[...])


def ret_out(o, proj, gn, w, gate, res, *, tm, row_of):
    m = o.shape[0]
    dv = RET_HEADS * RET_DV
    return pl.pallas_call(
        _ret_out_kernel,
        grid=(m // tm,),
        in_specs=[
            pl.BlockSpec((tm, dv), lambda i: (i, 0)),
            pl.BlockSpec((tm, dv), lambda i: (i, 2)),
            _vec_spec(dv),
            pl.BlockSpec((dv, D), lambda i: (0, 0)),
            _mod_spec(row_of),
            pl.BlockSpec((tm, D), lambda i: (i, 0)),
        ],
        out_specs=pl.BlockSpec((tm, D), lambda i: (i, 0)),
        out_shape=jax.ShapeDtypeStruct((m, D), F32),
        compiler_params=_params("parallel"),
        name="ret_out",
    )(o, proj, gn, w, gate, res)


def _pw1_glu_kernel(x_ref, sh_ref, sc_ref, gain_ref, w_ref, b_ref, o_ref):
    h = _mod_norm(x_ref[...], gain_ref[...], sh_ref[0], sc_ref[0]).astype(BF16)
    y = _dot(h, w_ref[...]) + b_ref[...]
    o_ref[...] = y[:, :D] * jax.nn.sigmoid(y[:, D:])


def pw1_glu(x, sh, sc, gain, w, b, *, tm, row_of):
    m = x.shape[0]
    return pl.pallas_call(
        _pw1_glu_kernel,
        grid=(m // tm,),
        in_specs=[
            pl.BlockSpec((tm, D), lambda i: (i, 0)),
            _mod_spec(row_of), _mod_spec(row_of), _vec_spec(D),
            pl.BlockSpec((D, 2 * D), lambda i: (0, 0)),
            _vec_spec(2 * D),
        ],
        out_specs=pl.BlockSpec((tm, D), lambda i: (i, 0)),
        out_shape=jax.ShapeDtypeStruct((m, D), F32),
        compiler_params=_params("parallel"),
        name="pw1_glu",
    )(x, sh, sc, gain, w, b)


def _dwconv_kernel(up_ref, u_ref, un_ref, cw_ref, cb_ref, lg_ref, lb_ref, w_ref, b_ref, gate_ref, res_ref,
                   o_ref, ue_scr, y_scr, *, tm, tiles_per_seq):
    t = pl.program_id(0) % tiles_per_seq
    hl = CONV_HALO
    ue_scr[0:hl] = jnp.where(t == 0, 0.0, up_ref[...])
    ue_scr[hl:hl + tm] = u_ref[...]
    ue_scr[hl + tm:] = jnp.where(t == tiles_per_seq - 1, 0.0, un_ref[...])

    base = hl - CONV_W // 2
    win = CONV_RB + 2 * hl
    for cblk in range(D // 128):
        cols = slice(cblk * 128, (cblk + 1) * 128)

        def row_block(rb, carry, cols=cols):
            r0 = pl.multiple_of(rb * CONV_RB, CONV_RB)
            window = ue_scr[pl.ds(r0, win), cols]
            acc = jnp.zeros((CONV_RB, 128), F32)
            for b in range(8):
                shifted = window if b == 0 else pltpu.roll(window, win - b, axis=0)
                for a in range((base + CONV_W - 1) // 8 + 1):
                    k = 8 * a + b - base
                    if 0 <= k < CONV_W:
                        acc = acc + cw_ref[k:k + 1, cols] * shifted[8 * a:8 * a + CONV_RB]
            y_scr[pl.ds(r0, CONV_RB), cols] = acc + cb_ref[:, cols]
            return carry

        lax.fori_loop(0, tm // CONV_RB, row_block, 0)

    y = y_scr[...]
    mu = jnp.mean(y, axis=-1, keepdims=True)
    yc = y - mu
    var = jnp.mean(yc * yc, axis=-1, keepdims=True)
    z = yc * lax.rsqrt(var + EPS) * lg_ref[...] + lb_ref[...]
    z = (z * jax.nn.sigmoid(z)).astype(BF16)
    o_ref[...] = res_ref[...] + gate_ref[0] * (_dot(z, w_ref[...]) + b_ref[...])


def dwconv_out(u, cw, cb, lg, lb, w, b, gate, res, *, tm, seq, row_of):
    m = u.shape[0]
    hb = tm // CONV_HALO
    last_hb = m // CONV_HALO - 1
    return pl.pallas_call(
        functools.partial(_dwconv_kernel, tm=tm, tiles_per_seq=seq // tm),
        grid=(m // tm,),
        in_specs=[
            pl.BlockSpec((CONV_HALO, D), lambda i: (jnp.maximum(i * hb - 1, 0), 0)),
            pl.BlockSpec((tm, D), lambda i: (i, 0)),
            pl.BlockSpec((CONV_HALO, D), lambda i: (jnp.minimum((i + 1) * hb, last_hb), 0)),
            pl.BlockSpec((CONV_W, D), lambda i: (0, 0)),
            _vec_spec(D), _vec_spec(D), _vec_spec(D),
            pl.BlockSpec((D, D), lambda i: (0, 0)),
            _vec_spec(D), _mod_spec(row_of),
            pl.BlockSpec((tm, D), lambda i: (i, 0)),
        ],
        out_specs=pl.BlockSpec((tm, D), lambda i: (i, 0)),
        out_shape=jax.ShapeDtypeStruct((m, D), F32),
        scratch_shapes=[pltpu.VMEM((tm + 2 * CONV_HALO, D), F32), pltpu.VMEM((tm, D), F32)],
        compiler_params=_params("parallel"),
        name="dwconv_out",
    )(u, u, u, cw, cb, lg, lb, w, b, gate, res)


def _rope_tables():
    t = np.arange(SEQ)
    row = (t // GRID_W).astype(np.float32)
    col = (t % GRID_W).astype(np.float32)
    n_freq = RET_DK // 4
    inv_freq = jnp.power(ROPE_BASE, -jnp.arange(n_freq, dtype=F32) / n_freq)
    ang = jnp.concatenate([row[:, None] * inv_freq, col[:, None] * inv_freq], axis=-1)
    return jnp.cos(ang), jnp.sin(ang)


def kernel(x, c, ctx, c_ctx, ada_w, ada_b, norm_mix, norm_ffn, na_w_qkv, na_w_o, na_q_gain, na_k_gain, na_rpb, ret_w_q, ret_w_k, ret_w_v, ret_w_g, ret_w_o, ret_gn_gain, ret_decay_logit, cv_w_pw1, cv_b_pw1, cv_w_dw, cv_b_dw, cv_ln_g, cv_ln_b, cv_w_pw2, cv_b_pw2, ffn_w_up, ffn_w_dw, ffn_b_dw, ffn_w_down):
    xs = x.reshape(BATCH * SEQ, D)
    cs = ctx.reshape(BATCH * CTX, D)
    tm_x, tm_c = 512, CTX
    tiles_per_batch = SEQ // tm_x

    def x_row(i):
        return i // tiles_per_batch

    def c_row(i):
        return CTX_ROW

    c_rows = jnp.zeros((MOD_ROWS, D), F32).at[:BATCH].set(c).at[CTX_ROW].set(c_ctx)
    mods = ada_table(c_rows, ada_w, ada_b).reshape(DEPTH, MOD_ROWS, 6, 1, D)
    zero_bias = jnp.zeros((1, D), F32)
    cos, sin = _rope_tables()

    for i in range(DEPTH):
        need_ctx = i < DEPTH - 1
        mod = [mods[i, :, k] for k in range(6)]
        g_mix = norm_mix[i].reshape(1, D)
        g_ffn = norm_ffn[i].reshape(1, D)
        kind, j = i % 3, i // 3
        if kind == 0:
            w = na_w_qkv[j].astype(BF16)
            qg = (jnp.tile(na_q_gain[j], NA_HEADS) * (NA_SCALE * LOG2E)).reshape(1, D)
            kg = jnp.tile(na_k_gain[j], NA_HEADS).reshape(1, D)
            qkv_x = na_qkv(xs, mod[0], mod[1], g_mix, w, qg, kg, tm=tm_x, row_of=x_row)
            qkv_c = na_qkv(cs, mod[0], mod[1], g_mix, w, qg, kg, tm=tm_c, row_of=c_row)
            ox, oc = na_attention(qkv_x, qkv_c, na_bias_table(na_rpb[j]), need_ctx=need_ctx)
            w_o = na_w_o[j].astype(BF16)
            xs = proj_residual(ox, w_o, zero_bias, mod[2], xs, tm=tm_x, row_of=x_row)
            if need_ctx:
                cs = proj_residual(oc, w_o, zero_bias, mod[2], cs, tm=tm_c, row_of=c_row)
        elif kind == 1:
            w = jnp.concatenate([ret_w_q[j], ret_w_k[j], ret_w_v[j], ret_w_g[j]], axis=1).astype(BF16)
            px = ret_proj(xs, mod[0], mod[1], g_mix, w, cos, sin, tm=tm_x, seq=SEQ, row_of=x_row, rope=True)
            pc = ret_proj(cs, mod[0], mod[1], g_mix, w, cos[:CTX], sin[:CTX], tm=tm_c, seq=CTX, row_of=c_row,
                          rope=False)
            dl = jnp.broadcast_to(ret_decay_logit[j][:, :, None, None], (2, RET_HEADS, 8, 128))
            ox, oc = ret_scan(px, pc, dl)
            w_o = ret_w_o[j].astype(BF16)
            gn = ret_gn_gain[j].reshape(1, RET_HEADS * RET_DV)
            xs = ret_out(ox, px, gn, w_o, mod[2], xs, tm=tm_x, row_of=x_row)
            if need_ctx:
                cs = ret_out(oc, pc, gn, w_o, mod[2], cs, tm=tm_c, row_of=c_row)
        else:
            w1 = cv_w_pw1[j].astype(BF16)
            b1 = cv_b_pw1[j].reshape(1, 2 * D)
            w2 = cv_w_pw2[j].astype(BF16)
            conv_args = (cv_w_dw[j], cv_b_dw[j].reshape(1, D), cv_ln_g[j].reshape(1, D), cv_ln_b[j].reshape(1, D),
                         w2, cv_b_pw2[j].reshape(1, D))
            ux = pw1_glu(xs, mod[0], mod[1], g_mix, w1, b1, tm=tm_x, row_of=x_row)
            xs = dwconv_out(ux, *conv_args, mod[2], xs, tm=tm_x, seq=SEQ, row_of=x_row)
            if need_ctx:
                uc = pw1_glu(cs, mod[0], mod[1], g_mix, w1, b1, tm=tm_c, row_of=c_row)
                cs = dwconv_out(uc, *conv_args, mod[2], cs, tm=tm_c, seq=CTX, row_of=c_row)

        w_up = ffn_w_up[i].astype(BF16)
        cw = ffn_w_dw[i]
        cb = ffn_b_dw[i].reshape(1, FFN_H)
        wd = ffn_w_down[i].astype(BF16)
        xs = conv_ffn(xs, mod[3], mod[4], mod[5], g_ffn, w_up, cw, cb, wd, tm=tm_x, seq=SEQ, row_of=x_row)
        if need_ctx:
            cs = conv_ffn(cs, mod[3], mod[4], mod[5], g_ffn, w_up, cw, cb, wd, tm=tm_c, seq=CTX, row_of=c_row)

    return xs.reshape(BATCH, SEQ, D)
```

```python
import functools

import numpy as np
import jax
import jax.numpy as jnp
from jax import lax
from jax.experimental import pallas as pl
from jax.experimental.pallas import tpu as pltpu

D = 1024
BATCH = 4
SEQ = 4096
DEPTH = 4
CTX = 256
GRID_W = 64
ROWS = SEQ // GRID_W
EPS = 1e-6

NA_HEADS = 16
NA_DH = 64
NA_KH = 8
NA_KW = 16
NA_SCALE = NA_DH ** -0.5
NA_PAIRS = NA_HEADS // 2
NA_WIN = NA_KH * GRID_W
NA_GROUP = 2
NA_CB = NA_KW
NA_NCB = GRID_W // NA_CB

RET_HEADS = 4
RET_DK = 256
RET_DV = 512
RET_C = 256
assert CTX == RET_C and (SEQ // RET_C) % 2 == 0
ROPE_BASE = 10000.0

CONV_W = 31
CONV_HALO = 16
CONV_RB = 64
FFN_H = 2816
FFN_SPLITS = (0, 1536, FFN_H)

MOD_ROWS = 8
CTX_ROW = BATCH
NEG = -1e30
LOG2E = 1.4426950408889634

F32 = jnp.float32
BF16 = jnp.bfloat16

VMEM_LIMIT = 52 * 1024 * 1024


def _params(*sem):
    return pltpu.CompilerParams(dimension_semantics=sem, vmem_limit_bytes=VMEM_LIMIT)


def _mod_norm(x, gain, shift, scale):
    ms = jnp.mean(x * x, axis=-1, keepdims=True)
    return x * lax.rsqrt(ms + EPS) * gain * (1.0 + scale) + shift


def _dot(a, b):
    return jnp.dot(a, b, preferred_element_type=F32)


def _dot_nt(a, b):
    return lax.dot_general(a, b, (((1,), (1,)), ((), ())), preferred_element_type=F32)


def _dot_tn(a, b):
    return lax.dot_general(a, b, (((0,), (0,)), ((), ())), preferred_element_type=F32)


def _ada_kernel(c_ref, w_ref, b_ref, o_ref):
    c = c_ref[...]
    s = (c * jax.nn.sigmoid(c)).astype(BF16)
    o_ref[0] = _dot(s, w_ref[0].astype(BF16)) + b_ref[0]


def ada_table(c_rows, ada_w, ada_b):
    tn = 1536
    return pl.pallas_call(
        _ada_kernel,
        grid=(DEPTH, 6 * D // tn),
        in_specs=[
            pl.BlockSpec((MOD_ROWS, D), lambda l, j: (0, 0)),
            pl.BlockSpec((1, D, tn), lambda l, j: (l, 0, j)),
            pl.BlockSpec((1, 1, tn), lambda l, j: (l, 0, j)),
        ],
        out_specs=pl.BlockSpec((1, MOD_ROWS, tn), lambda l, j: (l, 0, j)),
        out_shape=jax.ShapeDtypeStruct((DEPTH, MOD_ROWS, 6 * D), F32),
        compiler_params=_params("parallel", "parallel"),
        name="ada_table",
    )(c_rows, ada_w, ada_b.reshape(DEPTH, 1, 6 * D))


def _mod_spec(row_of):
    return pl.BlockSpec((1, 1, D), lambda i, *_: (row_of(i), 0, 0))


def _vec_spec(n):
    return pl.BlockSpec((1, n), lambda i, *_: (0, 0))


def _na_qkv_kernel(x_ref, sh_ref, sc_ref, gain_ref, w_ref, qg_ref, kg_ref, o_ref):
    h = _mod_norm(x_ref[...], gain_ref[...], sh_ref[0], sc_ref[0]).astype(BF16)
    r = lax.broadcasted_iota(jnp.int32, (128, 128), 0) // NA_DH
    c = lax.broadcasted_iota(jnp.int32, (128, 128), 1) // NA_DH
    seg = jnp.where(r == c, 1.0, 0.0).astype(BF16)
    for sec, g_ref in ((0, qg_ref), (1, kg_ref)):
        y = _dot(h, w_ref[:, sec * D:(sec + 1) * D])
        for g in range(D // 128):
            yg = y[:, g * 128:(g + 1) * 128]
            ss = _dot((yg * yg).astype(BF16), seg)
            yn = yg * lax.rsqrt(ss * (1.0 / NA_DH) + EPS) * g_ref[:, g * 128:(g + 1) * 128]
            o_ref[:, sec * D + g * 128: sec * D + (g + 1) * 128] = yn.astype(o_ref.dtype)
    o_ref[:, 2 * D:] = _dot(h, w_ref[:, 2 * D:]).astype(o_ref.dtype)


def na_qkv(x, sh, sc, gain, w, qg, kg, *, tm, row_of):
    m = x.shape[0]
    return pl.pallas_call(
        _na_qkv_kernel,
        grid=(m // tm,),
        in_specs=[
            pl.BlockSpec((tm, D), lambda i: (i, 0)),
            _mod_spec(row_of), _mod_spec(row_of), _vec_spec(D),
            pl.BlockSpec((D, 3 * D), lambda i: (0, 0)),
            _vec_spec(D), _vec_spec(D),
        ],
        out_specs=pl.BlockSpec((tm, 3 * D), lambda i: (i, 0)),
        out_shape=jax.ShapeDtypeStruct((m, 3 * D), BF16),
        compiler_params=_params("parallel"),
        name="na_qkv",
    )(x, sh, sc, gain, w, qg, kg)


def _split_heads(q, in_a):
    zero = jnp.zeros_like(q)
    return jnp.concatenate([jnp.where(in_a, q, zero), jnp.where(in_a, zero, q)], axis=0)


def _live_key_blocks(group):
    lo = min(max(8 * group - NA_KW // 2, 0), GRID_W - NA_KW)
    hi = min(max(8 * group + 7 - NA_KW // 2, 0), GRID_W - NA_KW) + NA_KW - 1
    return lo // NA_CB, hi // NA_CB + 1


def _na_attn_kernel(q_ref, k_ref, v_ref, kc_ref, vc_ref, bias_ref, *rest, need_ctx):
    if need_ctx:
        qc_ref, o_ref, oc_ref, kb_scr, vb_scr, s_scr, p_scr, l_scr = rest
    else:
        o_ref, kb_scr, vb_scr, s_scr, p_scr, l_scr = rest
    in_a = lax.broadcasted_iota(jnp.int32, (1, 128), 1) < NA_DH
    kc = kc_ref[...]
    vc = vc_ref[...]
    blk = NA_KH * NA_CB

    def relayout(row, carry):
        dst = pl.ds(pl.multiple_of(row * NA_CB, NA_CB), NA_CB)
        for cb in range(NA_NCB):
            src = pl.ds(pl.multiple_of(row * GRID_W + cb * NA_CB, NA_CB), NA_CB)
            kb_scr[cb, dst, :] = k_ref[src, :]
            vb_scr[cb, dst, :] = v_ref[src, :]
        return carry

    lax.fori_loop(0, ROWS, relayout, 0)
    p_scr[...] = jnp.zeros_like(p_scr)

    def window(r):
        if isinstance(r, int):
            rs = min(max(r - NA_KH // 2, 0), ROWS - NA_KH)
            return r * GRID_W, rs * NA_CB, r - rs
        rs = jnp.clip(r - NA_KH // 2, 0, ROWS - NA_KH)
        return pl.multiple_of(r * GRID_W, GRID_W), pl.multiple_of(rs * NA_CB, NA_CB), r - rs

    def key_window(ref, w0):
        return jnp.concatenate([ref[cb, pl.ds(w0, blk), :] for cb in range(NA_NCB)], axis=0)

    def live(group):
        b0, b1 = _live_key_blocks(group % (GRID_W // 8))
        return slice(8 * group, 8 * group + 8), slice(b0 * blk, b1 * blk)

    n_groups8 = 2 * GRID_W // 8

    def scores(r, slot):
        q0, w0, var = window(r)
        q2 = _split_heads(q_ref[pl.ds(q0, GRID_W), :], in_a)
        s_loc = _dot_nt(q2, key_window(kb_scr, w0))
        for g in range(n_groups8):
            rows, lanes = live(g)
            s_scr[slot, rows, lanes] = s_loc[rows, lanes] + bias_ref[0, var, rows, lanes]
        s_scr[slot, :, NA_WIN:] = _dot_nt(q2, kc)

    def softmax(slot):
        for g in range(n_groups8):
            rows, lanes = live(g)
            s = jnp.concatenate([s_scr[slot, rows, lanes], s_scr[slot, rows, NA_WIN:]], axis=1)
            p = jnp.exp2(s - jnp.max(s, axis=-1, keepdims=True))
            l_scr[slot, rows, :] = jnp.broadcast_to(jnp.sum(p, axis=-1, keepdims=True), (8, 128))
            n_loc = lanes.stop - lanes.start
            p_scr[slot, rows, lanes] = p[:, :n_loc]
            p_scr[slot, rows, NA_WIN:] = p[:, n_loc:]

    def values(r, slot):
        q0, w0, _ = window(r)
        p = p_scr[slot].astype(BF16)
        v = jnp.concatenate([key_window(vb_scr, w0), vc], axis=0)
        o2 = _dot(p, v) / l_scr[slot]
        o_ref[pl.ds(q0, GRID_W), :] = jnp.where(in_a, o2[:GRID_W], o2[GRID_W:]).astype(o_ref.dtype)

    def stage_values(t):
        for u in range(NA_GROUP):
            values(NA_GROUP * t + u, NA_GROUP * (t % 2) + u)

    def stage_softmax(t):
        for u in range(NA_GROUP):
            softmax(NA_GROUP * (t % 2) + u)

    def stage_scores(t):
        for u in range(NA_GROUP):
            scores(NA_GROUP * t + u, NA_GROUP * (t % 2) + u)

    n_groups = ROWS // NA_GROUP
    stage_scores(0)
    stage_softmax(0)
    stage_scores(1)

    def step(t, carry):
        stage_values(t - 2)
        stage_softmax(t - 1)
        stage_scores(t)
        return carry

    lax.fori_loop(2, n_groups, step, 0)
    stage_values(n_groups - 2)
    stage_softmax(n_groups - 1)
    stage_values(n_groups - 1)

    if need_ctx:
        q2 = _split_heads(qc_ref[...], in_a)
        s = _dot_nt(q2, kc)
        p = jnp.exp2(s - jnp.max(s, axis=-1, keepdims=True))
        o2 = _dot(p.astype(BF16), vc) / jnp.sum(p, axis=-1, keepdims=True)
        oc_ref[...] = jnp.where(in_a, o2[:CTX], o2[CTX:]).astype(oc_ref.dtype)


def na_attention(qkv_x, qkv_c, bias, *, need_ctx):
    def col(sec):
        return lambda p, b: (b, sec * NA_PAIRS + p)
    in_specs = [
        pl.BlockSpec((SEQ, 128), col(0)),
        pl.BlockSpec((SEQ, 128), col(1)),
        pl.BlockSpec((SEQ, 128), col(2)),
        pl.BlockSpec((CTX, 128), col(1)),
        pl.BlockSpec((CTX, 128), col(2)),
        pl.BlockSpec((1, NA_KH, 128, NA_WIN), lambda p, b: (p, 0, 0, 0)),
    ]
    args = [qkv_x, qkv_x, qkv_x, qkv_c, qkv_c, bias]
    out_specs = [pl.BlockSpec((SEQ, 128), lambda p, b: (b, p))]
    out_shape = [jax.ShapeDtypeStruct((BATCH * SEQ, D), BF16)]
    if need_ctx:
        in_specs.append(pl.BlockSpec((CTX, 128), col(0)))
        args.append(qkv_c)
        out_specs.append(pl.BlockSpec((CTX, 128), lambda p, b: (b, p)))
        out_shape.append(jax.ShapeDtypeStruct((BATCH * CTX, D), BF16))
    out = pl.pallas_call(
        functools.partial(_na_attn_kernel, need_ctx=need_ctx),
        grid=(NA_PAIRS, BATCH),
        in_specs=in_specs,
        out_specs=out_specs,
        out_shape=out_shape,
        scratch_shapes=[
            pltpu.VMEM((NA_NCB, ROWS * NA_CB, 128), BF16),
            pltpu.VMEM((NA_NCB, ROWS * NA_CB, 128), BF16),
            pltpu.VMEM((2 * NA_GROUP, 2 * GRID_W, NA_WIN + CTX), F32),
            pltpu.VMEM((2 * NA_GROUP, 2 * GRID_W, NA_WIN + CTX), F32),
            pltpu.VMEM((2 * NA_GROUP, 2 * GRID_W, 128), F32),
        ],
        compiler_params=_params("parallel", "parallel"),
        name="na_attention",
    )(*args)
    return (out[0], out[1]) if need_ctx else (out[0], None)


def na_bias_table(rpb):
    c = np.arange(GRID_W)[:, None]
    kc = np.arange(GRID_W)[None, :]
    cs = np.clip(c - NA_KW // 2, 0, GRID_W - NA_KW)
    valid = (kc >= cs) & (kc < cs + NA_KW)
    lpad = GRID_W - NA_KW
    p = jnp.pad(rpb, ((0, 0), (0, 0), (lpad, 128 - lpad - (2 * NA_KW - 1))))
    toep = jnp.tile(p, (1, 1, GRID_W))[:, :, :GRID_W * 127].reshape(NA_HEADS, 2 * NA_KH - 1, GRID_W, 127)
    toep = jnp.where(valid, toep[..., GRID_W - 1:] * LOG2E, NEG)
    t = jnp.stack([toep[:, NA_KH - 1 - var: 2 * NA_KH - 1 - var] for var in range(NA_KH)], axis=1)
    t = t.reshape(NA_HEADS, NA_KH, NA_KH, GRID_W, NA_NCB, NA_CB)
    t = t.transpose(0, 1, 3, 4, 2, 5).reshape(NA_PAIRS, 2, NA_KH, GRID_W, NA_WIN)
    return t.transpose(0, 2, 1, 3, 4).reshape(NA_PAIRS, NA_KH, 2 * GRID_W, NA_WIN)


def _proj_res_kernel(a_ref, w_ref, b_ref, gate_ref, res_ref, o_ref):
    y = _dot(a_ref[...], w_ref[...]) + b_ref[...]
    o_ref[...] = res_ref[...] + gate_ref[0] * y


def proj_residual(a, w, b, gate, res, *, tm, row_of):
    m, k = a.shape
    return pl.pallas_call(
        _proj_res_kernel,
        grid=(m // tm,),
        in_specs=[
            pl.BlockSpec((tm, k), lambda i: (i, 0)),
            pl.BlockSpec((k, D), lambda i: (0, 0)),
            _vec_spec(D), _mod_spec(row_of),
            pl.BlockSpec((tm, D), lambda i: (i, 0)),
        ],
        out_specs=pl.BlockSpec((tm, D), lambda i: (i, 0)),
        out_shape=jax.ShapeDtypeStruct((m, D), F32),
        compiler_params=_params("parallel"),
        name="proj_residual",
    )(a, w, b, gate, res)


def _ffn_kernel(xp_ref, x_ref, xn_ref, sh_ref, sc_ref, gate_ref, gain_ref, wup_ref, cw_ref, cb_ref, wd_ref,
                o_ref, h_scr, *, tm, tiles_per_seq):
    hl = 16
    gain, shift, scale = gain_ref[...], sh_ref[0], sc_ref[0]
    t = pl.program_id(0) % tiles_per_seq
    hp = _mod_norm(xp_ref[...], gain, shift, scale)
    hn = _mod_norm(xn_ref[...], gain, shift, scale)
    h_scr[0:hl] = jnp.where(t == 0, 0.0, hp).astype(BF16)
    h_scr[hl:hl + tm] = _mod_norm(x_ref[...], gain, shift, scale).astype(BF16)
    h_scr[hl + tm:] = jnp.where(t == tiles_per_seq - 1, 0.0, hn).astype(BF16)

    acc = None
    for c0, c1 in zip(FFN_SPLITS[:-1], FFN_SPLITS[1:]):
        u = _dot(h_scr[...], wup_ref[:, c0:c1])
        v = _dot(h_scr[hl:hl + tm], wup_ref[:, FFN_H + c0:FFN_H + c1])
        uc = (cw_ref[0:1, c0:c1] * u[hl - 1:hl - 1 + tm] + cw_ref[1:2, c0:c1] * u[hl:hl + tm]
              + cw_ref[2:3, c0:c1] * u[hl + 1:hl + 1 + tm] + cb_ref[:, c0:c1])
        g = (jax.nn.gelu(uc, approximate=True) * v).astype(BF16)
        part = _dot(g, wd_ref[c0:c1, :])
        acc = part if acc is None else acc + part
    o_ref[...] = x_ref[...] + gate_ref[0] * acc


def conv_ffn(x, sh, sc, gate, gain, w_up, cw, cb, wd, *, tm, seq, row_of):
    m = x.shape[0]
    hb = tm // 16
    last_hb = m // 16 - 1

    def resident(shape):
        return pl.BlockSpec(shape, lambda i: (0, 0), pipeline_mode=pl.Buffered(1))

    return pl.pallas_call(
        functools.partial(_ffn_kernel, tm=tm, tiles_per_seq=seq // tm),
        grid=(m // tm,),
        in_specs=[
            pl.BlockSpec((16, D), lambda i: (jnp.maximum(i * hb - 1, 0), 0)),
            pl.BlockSpec((tm, D), lambda i: (i, 0)),
            pl.BlockSpec((16, D), lambda i: (jnp.minimum((i + 1) * hb, last_hb), 0)),
            _mod_spec(row_of), _mod_spec(row_of), _mod_spec(row_of), _vec_spec(D),
            resident((D, 2 * FFN_H)), resident((3, FFN_H)), resident((1, FFN_H)), resident((FFN_H, D)),
        ],
        out_specs=pl.BlockSpec((tm, D), lambda i: (i, 0)),
        out_shape=jax.ShapeDtypeStruct((m, D), F32),
        scratch_shapes=[pltpu.VMEM((tm + 32, D), BF16)],
        compiler_params=_params("parallel"),
        name="conv_ffn",
    )(x, x, x, sh, sc, gate, gain, w_up, cw, cb, wd)


def _ret_proj_kernel(x_ref, sh_ref, sc_ref, gain_ref, w_ref, cos_ref, sin_ref, o_ref, *, rope):
    h = _mod_norm(x_ref[...], gain_ref[...], sh_ref[0], sc_ref[0]).astype(BF16)
    qk_width = 2 * RET_HEADS * RET_DK
    v_end = qk_width + RET_HEADS * RET_DV
    k_scale = RET_DK ** -0.5
    for hh in range(2 * RET_HEADS):
        y = _dot(h, w_ref[:, hh * RET_DK:(hh + 1) * RET_DK])
        s = k_scale if hh >= RET_HEADS else 1.0
        x1, x2 = y[:, :128], y[:, 128:]
        if rope:
            cos, sin = cos_ref[...], sin_ref[...]
            x1, x2 = x1 * cos - x2 * sin, x1 * sin + x2 * cos
        o_ref[:, hh * RET_DK: hh * RET_DK + 128] = (x1 * s).astype(o_ref.dtype)
        o_ref[:, hh * RET_DK + 128: (hh + 1) * RET_DK] = (x2 * s).astype(o_ref.dtype)
    o_ref[:, qk_width:v_end] = _dot(h, w_ref[:, qk_width:v_end]).astype(o_ref.dtype)
    g = _dot(h, w_ref[:, v_end:])
    o_ref[:, v_end:] = (g * jax.nn.sigmoid(g)).astype(o_ref.dtype)


def ret_proj(x, sh, sc, gain, w, cos, sin, *, tm, seq, row_of, rope):
    m = x.shape[0]
    n = w.shape[1]
    tps = seq // tm
    return pl.pallas_call(
        functools.partial(_ret_proj_kernel, rope=rope),
        grid=(m // tm,),
        in_specs=[
            pl.BlockSpec((tm, D), lambda i: (i, 0)),
            _mod_spec(row_of), _mod_spec(row_of), _vec_spec(D),
            pl.BlockSpec((D, n), lambda i: (0, 0), pipeline_mode=pl.Buffered(1)),
            pl.BlockSpec((tm, 128), lambda i: (i % tps, 0)),
            pl.BlockSpec((tm, 128), lambda i: (i % tps, 0)),
        ],
        out_specs=pl.BlockSpec((tm, n), lambda i: (i, 0)),
        out_shape=jax.ShapeDtypeStruct((m, n), BF16),
        compiler_params=_params("parallel"),
        name="ret_proj",
    )(x, sh, sc, gain, w, cos, sin)


def _log_sigmoid(x):
    return jnp.minimum(x, 0.0) - jnp.log1p(jnp.exp(-jnp.abs(x)))


def _ret_scan_kernel(qx_ref, kx_ref, vx_ref, qc_ref, kc_ref, vc_ref, dl_ref, ox_ref, oc_ref,
                     sf_scr, sb_scr, ax_scr, mask_scr):
    C = RET_C
    n = SEQ // C
    pos = lax.broadcasted_iota(jnp.int32, (C, 128), 0).astype(F32)
    ii = lax.broadcasted_iota(jnp.int32, (C, C), 0).astype(F32)
    jj = lax.broadcasted_iota(jnp.int32, (C, C), 1).astype(F32)

    def lanes(x, width):
        return jnp.concatenate([x] * (width // 128), axis=1)

    def decays(direction):
        lg = jnp.broadcast_to(_log_sigmoid(dl_ref[direction, 0])[0:1, :], (C, 128))
        diff = ii - jj if direction == 0 else jj - ii
        p = pos if direction == 0 else (C - 1.0) - pos
        intra = jnp.where(diff >= 0, jnp.exp(jnp.maximum(diff, 0.0) * lanes(lg, C)), 0.0)
        qd = jnp.exp((p + 1.0) * lg)
        kd = jnp.exp((C - 1.0 - p) * lg)
        cd = jnp.exp(C * lg)[0:1]
        return intra, qd, kd, cd

    intra_f, qd_f, kd_f, cd_f = decays(0)
    intra_b, qd_b, kd_b, cd_b = decays(1)
    mask_scr[...] = intra_f + intra_b

    def group_norm(o):
        mu = jnp.mean(o, axis=-1, keepdims=True)
        oc = o - mu
        return oc * lax.rsqrt(jnp.mean(oc * oc, axis=-1, keepdims=True) + EPS)

    def intra_term(q, k, v):
        return _dot((_dot_nt(q, k) * mask_scr[...]).astype(BF16), v)

    def cross_term(q, s_scr, qd):
        return lanes(qd, RET_DV) * _dot(q, s_scr[...].astype(BF16))

    def state_term(k, v, kd):
        return _dot_tn((k.astype(F32) * lanes(kd, RET_DK)).astype(BF16), v)

    def advance(s_scr, k, v, kd, cd):
        s_scr[...] = s_scr[...] * lanes(cd, RET_DV) + state_term(k, v, kd)

    q, k, v = qc_ref[...], kc_ref[...], vc_ref[...]
    oc_ref[...] = group_norm(intra_term(q, k, v)).astype(oc_ref.dtype)
    sf_scr[...] = state_term(k, v, kd_f)
    sb_scr[...] = state_term(k, v, kd_b)

    def visit(t, finish):
        rf = pl.multiple_of(t * C, C)
        rb = pl.multiple_of((n - 1 - t) * C, C)
        q, k, v = qx_ref[pl.ds(rf, C), :], kx_ref[pl.ds(rf, C), :], vx_ref[pl.ds(rf, C), :]
        o = intra_term(q, k, v) + cross_term(q, sf_scr, qd_f)
        if finish:
            ox_ref[pl.ds(rf, C), :] = group_norm(ax_scr[pl.ds(rf, C), :] + o).astype(ox_ref.dtype)
        else:
            ax_scr[pl.ds(rf, C), :] = o
        advance(sf_scr, k, v, kd_f, cd_f)
        q, k, v = qx_ref[pl.ds(rb, C), :], kx_ref[pl.ds(rb, C), :], vx_ref[pl.ds(rb, C), :]
        o = cross_term(q, sb_scr, qd_b)
        if finish:
            ox_ref[pl.ds(rb, C), :] = group_norm(ax_scr[pl.ds(rb, C), :] + o).astype(ox_ref.dtype)
        else:
            ax_scr[pl.ds(rb, C), :] = o
        advance(sb_scr, k, v, kd_b, cd_b)

    def first_visits(t, carry):
        visit(t, False)
        return carry

    def second_visits(t, carry):
        visit(t, True)
        return carry

    lax.fori_loop(0, n // 2, first_visits, 0)
    lax.fori_loop(n // 2, n, second_visits, 0)


def ret_scan(px, pc, dl_tiles):
    kb = RET_HEADS
    vb = 2 * RET_HEADS * RET_DK // RET_DV
    return pl.pallas_call(
        _ret_scan_kernel,
        grid=(BATCH, RET_HEADS),
        in_specs=[
            pl.BlockSpec((SEQ, RET_DK), lambda b, h: (b, h)),
            pl.BlockSpec((SEQ, RET_DK), lambda b, h: (b, kb + h)),
            pl.BlockSpec((SEQ, RET_DV), lambda b, h: (b, vb + h)),
            pl.BlockSpec((CTX, RET_DK), lambda b, h: (b, h)),
            pl.BlockSpec((CTX, RET_DK), lambda b, h: (b, kb + h)),
            pl.BlockSpec((CTX, RET_DV), lambda b, h: (b, vb + h)),
            pl.BlockSpec((2, 1, 8, 128), lambda b, h: (0, h, 0, 0)),
        ],
        out_specs=[
            pl.BlockSpec((SEQ, RET_DV), lambda b, h: (b, h)),
            pl.BlockSpec((CTX, RET_DV), lambda b, h: (b, h)),
        ],
        out_shape=[
            jax.ShapeDtypeStruct((BATCH * SEQ, RET_HEADS * RET_DV), BF16),
            jax.ShapeDtypeStruct((BATCH * CTX, RET_HEADS * RET_DV), BF16),
        ],
        scratch_shapes=[
            pltpu.VMEM((RET_DK, RET_DV), F32),
            pltpu.VMEM((RET_DK, RET_DV), F32),
            pltpu.VMEM((SEQ, RET_DV), F32),
            pltpu.VMEM((RET_C, RET_C), F32),
        ],
        compiler_params=_params("parallel", "parallel"),
        name="ret_scan",
    )(px, px, px, pc, pc, pc, dl_tiles)


def _ret_out_kernel(o_ref, g_ref, gn_ref, w_ref, gate_ref, res_ref, out_ref):
    a = (o_ref[...].astype(F32) * gn_ref[...] * g_ref[...].astype(F32)).astype(BF16)
    out_ref[...] = res_ref[...] + gate_ref[0] * _dot(a, w_ref[...])


def ret_out(o, proj, gn, w, gate, res, *, tm, row_of):
    m = o.shape[0]
    dv = RET_HEADS * RET_DV
    return pl.pallas_call(
        _ret_out_kernel,
        grid=(m // tm,),
        in_specs=[
            pl.BlockSpec((tm, dv), lambda i: (i, 0)),
            pl.BlockSpec((tm, dv), lambda i: (i, 2)),
            _vec_spec(dv),
            pl.BlockSpec((dv, D), lambda i: (0, 0)),
            _mod_spec(row_of),
            pl.BlockSpec((tm, D), lambda i: (i, 0)),
        ],
        out_specs=pl.BlockSpec((tm, D), lambda i: (i, 0)),
        out_shape=jax.ShapeDtypeStruct((m, D), F32),
        compiler_params=_params("parallel"),
        name="ret_out",
    )(o, proj, gn, w, gate, res)


def _pw1_glu_kernel(x_ref, sh_ref, sc_ref, gain_ref, w_ref, b_ref, o_ref):
    h = _mod_norm(x_ref[...], gain_ref[...], sh_ref[0], sc_ref[0]).astype(BF16)
    y = _dot(h, w_ref[...]) + b_ref[...]
    o_ref[...] = y[:, :D] * jax.nn.sigmoid(y[:, D:])


def pw1_glu(x, sh, sc, gain, w, b, *, tm, row_of):
    m = x.shape[0]
    return pl.pallas_call(
        _pw1_glu_kernel,
        grid=(m // tm,),
        in_specs=[
            pl.BlockSpec((tm, D), lambda i: (i, 0)),
            _mod_spec(row_of), _mod_spec(row_of), _vec_spec(D),
            pl.BlockSpec((D, 2 * D), lambda i: (0, 0)),
            _vec_spec(2 * D),
        ],
        out_specs=pl.BlockSpec((tm, D), lambda i: (i, 0)),
        out_shape=jax.ShapeDtypeStruct((m, D), F32),
        compiler_params=_params("parallel"),
        name="pw1_glu",
    )(x, sh, sc, gain, w, b)


def _dwconv_kernel(up_ref, u_ref, un_ref, cw_ref, cb_ref, lg_ref, lb_ref, w_ref, b_ref, gate_ref, res_ref,
                   o_ref, ue_scr, y_scr, *, tm, tiles_per_seq):
    t = pl.program_id(0) % tiles_per_seq
    hl = CONV_HALO
    ue_scr[0:hl] = jnp.where(t == 0, 0.0, up_ref[...])
    ue_scr[hl:hl + tm] = u_ref[...]
    ue_scr[hl + tm:] = jnp.where(t == tiles_per_seq - 1, 0.0, un_ref[...])

    base = hl - CONV_W // 2
    win = CONV_RB + 2 * hl
    for cblk in range(D // 128):
        cols = slice(cblk * 128, (cblk + 1) * 128)

        def row_block(rb, carry, cols=cols):
            r0 = pl.multiple_of(rb * CONV_RB, CONV_RB)
            window = ue_scr[pl.ds(r0, win), cols]
            acc = jnp.zeros((CONV_RB, 128), F32)
            for b in range(8):
                shifted = window if b == 0 else pltpu.roll(window, win - b, axis=0)
                for a in range((base + CONV_W - 1) // 8 + 1):
                    k = 8 * a + b - base
                    if 0 <= k < CONV_W:
                        acc = acc + cw_ref[k:k + 1, cols] * shifted[8 * a:8 * a + CONV_RB]
            y_scr[pl.ds(r0, CONV_RB), cols] = acc + cb_ref[:, cols]
            return carry

        lax.fori_loop(0, tm // CONV_RB, row_block, 0)

    y = y_scr[...]
    mu = jnp.mean(y, axis=-1, keepdims=True)
    yc = y - mu
    var = jnp.mean(yc * yc, axis=-1, keepdims=True)
    z = yc * lax.rsqrt(var + EPS) * lg_ref[...] + lb_ref[...]
    z = (z * jax.nn.sigmoid(z)).astype(BF16)
    o_ref[...] = res_ref[...] + gate_ref[0] * (_dot(z, w_ref[...]) + b_ref[...])


def dwconv_out(u, cw, cb, lg, lb, w, b, gate, res, *, tm, seq, row_of):
    m = u.shape[0]
    hb = tm // CONV_HALO
    last_hb = m // CONV_HALO - 1
    return pl.pallas_call(
        functools.partial(_dwconv_kernel, tm=tm, tiles_per_seq=seq // tm),
        grid=(m // tm,),
        in_specs=[
            pl.BlockSpec((CONV_HALO, D), lambda i: (jnp.maximum(i * hb - 1, 0), 0)),
            pl.BlockSpec((tm, D), lambda i: (i, 0)),
            pl.BlockSpec((CONV_HALO, D), lambda i: (jnp.minimum((i + 1) * hb, last_hb), 0)),
            pl.BlockSpec((CONV_W, D), lambda i: (0, 0)),
            _vec_spec(D), _vec_spec(D), _vec_spec(D),
            pl.BlockSpec((D, D), lambda i: (0, 0)),
            _vec_spec(D), _mod_spec(row_of),
            pl.BlockSpec((tm, D), lambda i: (i, 0)),
        ],
        out_specs=pl.BlockSpec((tm, D), lambda i: (i, 0)),
        out_shape=jax.ShapeDtypeStruct((m, D), F32),
        scratch_shapes=[pltpu.VMEM((tm + 2 * CONV_HALO, D), F32), pltpu.VMEM((tm, D), F32)],
        compiler_params=_params("parallel"),
        name="dwconv_out",
    )(u, u, u, cw, cb, lg, lb, w, b, gate, res)


def _rope_tables():
    t = np.arange(SEQ)
    row = (t // GRID_W).astype(np.float32)
    col = (t % GRID_W).astype(np.float32)
    n_freq = RET_DK // 4
    inv_freq = jnp.power(ROPE_BASE, -jnp.arange(n_freq, dtype=F32) / n_freq)
    ang = jnp.concatenate([row[:, None] * inv_freq, col[:, None] * inv_freq], axis=-1)
    return jnp.cos(ang), jnp.sin(ang)


def kernel(x, c, ctx, c_ctx, ada_w, ada_b, norm_mix, norm_ffn, na_w_qkv, na_w_o, na_q_gain, na_k_gain, na_rpb, ret_w_q, ret_w_k, ret_w_v, ret_w_g, ret_w_o, ret_gn_gain, ret_decay_logit, cv_w_pw1, cv_b_pw1, cv_w_dw, cv_b_dw, cv_ln_g, cv_ln_b, cv_w_pw2, cv_b_pw2, ffn_w_up, ffn_w_dw, ffn_b_dw, ffn_w_down):
    xs = x.reshape(BATCH * SEQ, D)
    cs = ctx.reshape(BATCH * CTX, D)
    tm_x, tm_c = 512, CTX
    tiles_per_batch = SEQ // tm_x

    def x_row(i):
        return i // tiles_per_batch

    def c_row(i):
        return CTX_ROW

    c_rows = jnp.zeros((MOD_ROWS, D), F32).at[:BATCH].set(c).at[CTX_ROW].set(c_ctx)
    mods = ada_table(c_rows, ada_w, ada_b).reshape(DEPTH, MOD_ROWS, 6, 1, D)
    zero_bias = jnp.zeros((1, D), F32)
    cos, sin = _rope_tables()

    for i in range(DEPTH):
        need_ctx = i < DEPTH - 1
        mod = [mods[i, :, k] for k in range(6)]
        g_mix = norm_mix[i].reshape(1, D)
        g_ffn = norm_ffn[i].reshape(1, D)
        kind, j = i % 3, i // 3
        if kind == 0:
            w = na_w_qkv[j].astype(BF16)
            qg = (jnp.tile(na_q_gain[j], NA_HEADS) * (NA_SCALE * LOG2E)).reshape(1, D)
            kg = jnp.tile(na_k_gain[j], NA_HEADS).reshape(1, D)
            qkv_x = na_qkv(xs, mod[0], mod[1], g_mix, w, qg, kg, tm=tm_x, row_of=x_row)
            qkv_c = na_qkv(cs, mod[0], mod[1], g_mix, w, qg, kg, tm=tm_c, row_of=c_row)
            ox, oc = na_attention(qkv_x, qkv_c, na_bias_table(na_rpb[j]), need_ctx=need_ctx)
            w_o = na_w_o[j].astype(BF16)
            xs = proj_residual(ox, w_o, zero_bias, mod[2], xs, tm=tm_x, row_of=x_row)
            if need_ctx:
                cs = proj_residual(oc, w_o, zero_bias, mod[2], cs, tm=tm_c, row_of=c_row)
        elif kind == 1:
            w = jnp.concatenate([ret_w_q[j], ret_w_k[j], ret_w_v[j], ret_w_g[j]], axis=1).astype(BF16)
            px = ret_proj(xs, mod[0], mod[1], g_mix, w, cos, sin, tm=tm_x, seq=SEQ, row_of=x_row, rope=True)
            pc = ret_proj(cs, mod[0], mod[1], g_mix, w, cos[:CTX], sin[:CTX], tm=tm_c, seq=CTX, row_of=c_row,
                          rope=False)
            dl = jnp.broadcast_to(ret_decay_logit[j][:, :, None, None], (2, RET_HEADS, 8, 128))
            ox, oc = ret_scan(px, pc, dl)
            w_o = ret_w_o[j].astype(BF16)
            gn = ret_gn_gain[j].reshape(1, RET_HEADS * RET_DV)
            xs = ret_out(ox, px, gn, w_o, mod[2], xs, tm=tm_x, row_of=x_row)
            if need_ctx:
                cs = ret_out(oc, pc, gn, w_o, mod[2], cs, tm=tm_c, row_of=c_row)
        else:
            w1 = cv_w_pw1[j].astype(BF16)
            b1 = cv_b_pw1[j].reshape(1, 2 * D)
            w2 = cv_w_pw2[j].astype(BF16)
            conv_args = (cv_w_dw[j], cv_b_dw[j].reshape(1, D), cv_ln_g[j].reshape(1, D), cv_ln_b[j].reshape(1, D),
                         w2, cv_b_pw2[j].reshape(1, D))
            ux = pw1_glu(xs, mod[0], mod[1], g_mix, w1, b1, tm=tm_x, row_of=x_row)
            xs = dwconv_out(ux, *conv_args, mod[2], xs, tm=tm_x, seq=SEQ, row_of=x_row)
            if need_ctx:
                uc = pw1_glu(cs, mod[0], mod[1], g_mix, w1, b1, tm=tm_c, row_of=c_row)
                cs = dwconv_out(uc, *conv_args, mod[2], cs, tm=tm_c, seq=CTX, row_of=c_row)

        w_up = ffn_w_up[i].astype(BF16)
        cw = ffn_w_dw[i]
        cb = ffn_b_dw[i].reshape(1, FFN_H)
        wd = ffn_w_down[i].astype(BF16)
        xs = conv_ffn(xs, mod[3], mod[4], mod[5], g_ffn, w_up, cw, cb, wd, tm=tm_x, seq=SEQ, row_of=x_row)
        if need_ctx:
            cs = conv_ffn(cs, mod[3], mod[4], mod[5], g_ffn, w_up, cw, cb, wd, tm=tm_c, seq=CTX, row_of=c_row)

    return xs.reshape(BATCH, SEQ, D)
```

```python
import functools

import numpy as np
import jax
import jax.numpy as jnp
from jax import lax
from jax.experimental import pallas as pl
from jax.experimental.pallas import tpu as pltpu

D = 1024
BATCH = 4
SEQ = 4096
DEPTH = 4
CTX = 256
GRID_W = 64
ROWS = SEQ // GRID_W
EPS = 1e-6

NA_HEADS = 16
NA_DH = 64
NA_KH = 8
NA_KW = 16
NA_SCALE = NA_DH ** -0.5
NA_PAIRS = NA_HEADS // 2
NA_WIN = NA_KH * GRID_W
NA_GROUP = 2
NA_CB = NA_KW
NA_NCB = GRID_W // NA_CB

RET_HEADS = 4
RET_DK = 256
RET_DV = 512
RET_C = 256
assert CTX == RET_C and (SEQ // RET_C) % 2 == 0
ROPE_BASE = 10000.0

CONV_W = 31
CONV_HALO = 16
CONV_RB = 64
FFN_H = 2816
FFN_SPLITS = (0, 1536, FFN_H)

MOD_ROWS = 8
CTX_ROW = BATCH
NEG = -1e30
LOG2E = 1.4426950408889634

F32 = jnp.float32
BF16 = jnp.bfloat16

VMEM_LIMIT = 52 * 1024 * 1024


def _params(*sem):
    return pltpu.CompilerParams(dimension_semantics=sem, vmem_limit_bytes=VMEM_LIMIT)


def _mod_norm(x, gain, shift, scale):
    ms = jnp.mean(x * x, axis=-1, keepdims=True)
    return x * lax.rsqrt(ms + EPS) * gain * (1.0 + scale) + shift


def _dot(a, b):
    return jnp.dot(a, b, preferred_element_type=F32)


def _dot_nt(a, b):
    return lax.dot_general(a, b, (((1,), (1,)), ((), ())), preferred_element_type=F32)


def _dot_tn(a, b):
    return lax.dot_general(a, b, (((0,), (0,)), ((), ())), preferred_element_type=F32)


def _ada_kernel(c_ref, w_ref, b_ref, o_ref):
    c = c_ref[...]
    s = (c * jax.nn.sigmoid(c)).astype(BF16)
    o_ref[0] = _dot(s, w_ref[0].astype(BF16)) + b_ref[0]


def ada_table(c_rows, ada_w, ada_b):
    tn = 1536
    return pl.pallas_call(
        _ada_kernel,
        grid=(DEPTH, 6 * D // tn),
        in_specs=[
            pl.BlockSpec((MOD_ROWS, D), lambda l, j: (0, 0)),
            pl.BlockSpec((1, D, tn), lambda l, j: (l, 0, j)),
            pl.BlockSpec((1, 1, tn), lambda l, j: (l, 0, j)),
        ],
        out_specs=pl.BlockSpec((1, MOD_ROWS, tn), lambda l, j: (l, 0, j)),
        out_shape=jax.ShapeDtypeStruct((DEPTH, MOD_ROWS, 6 * D), F32),
        compiler_params=_params("parallel", "parallel"),
        name="ada_table",
    )(c_rows, ada_w, ada_b.reshape(DEPTH, 1, 6 * D))


def _mod_spec(row_of):
    return pl.BlockSpec((1, 1, D), lambda i, *_: (row_of(i), 0, 0))


def _vec_spec(n):
    return pl.BlockSpec((1, n), lambda i, *_: (0, 0))


def _na_qkv_kernel(x_ref, sh_ref, sc_ref, gain_ref, w_ref, qg_ref, kg_ref, o_ref):
    h = _mod_norm(x_ref[...], gain_ref[...], sh_ref[0], sc_ref[0]).astype(BF16)
    r = lax.broadcasted_iota(jnp.int32, (128, 128), 0) // NA_DH
    c = lax.broadcasted_iota(jnp.int32, (128, 128), 1) // NA_DH
    seg = jnp.where(r == c, 1.0, 0.0).astype(BF16)
    for sec, g_ref in ((0, qg_ref), (1, kg_ref)):
        y = _dot(h, w_ref[:, sec * D:(sec + 1) * D])
        for g in range(D // 128):
            yg = y[:, g * 128:(g + 1) * 128]
            ss = _dot((yg * yg).astype(BF16), seg)
            yn = yg * lax.rsqrt(ss * (1.0 / NA_DH) + EPS) * g_ref[:, g * 128:(g + 1) * 128]
            o_ref[:, sec * D + g * 128: sec * D + (g + 1) * 128] = yn.astype(o_ref.dtype)
    o_ref[:, 2 * D:] = _dot(h, w_ref[:, 2 * D:]).astype(o_ref.dtype)


def na_qkv(x, sh, sc, gain, w, qg, kg, *, layer, tm, row_of):
    m = x.shape[0]
    return pl.pallas_call(
        _na_qkv_kernel,
        grid=(m // tm,),
        in_specs=[
            pl.BlockSpec((tm, D), lambda i: (i, 0)),
            _mod_spec(row_of), _mod_spec(row_of), _vec_spec(D),
            pl.BlockSpec((None, D, 3 * D), lambda i: (layer, 0, 0)),
            _vec_spec(D), _vec_spec(D),
        ],
        out_specs=pl.BlockSpec((tm, 3 * D), lambda i: (i, 0)),
        out_shape=jax.ShapeDtypeStruct((m, 3 * D), BF16),
        compiler_params=_params("parallel"),
        name="na_qkv",
    )(x, sh, sc, gain, w, qg, kg)


def _split_heads(q, in_a):
    zero = jnp.zeros_like(q)
    return jnp.concatenate([jnp.where(in_a, q, zero), jnp.where(in_a, zero, q)], axis=0)


def _live_key_blocks(group):
    lo = min(max(8 * group - NA_KW // 2, 0), GRID_W - NA_KW)
    hi = min(max(8 * group + 7 - NA_KW // 2, 0), GRID_W - NA_KW) + NA_KW - 1
    return lo // NA_CB, hi // NA_CB + 1


def _na_attn_kernel(q_ref, k_ref, v_ref, kc_ref, vc_ref, rpb_ref, *rest, need_ctx):
    if need_ctx:
        qc_ref, o_ref, oc_ref, kb_scr, vb_scr, s_scr, p_scr, l_scr, toep_scr, bias_scr = rest
    else:
        o_ref, kb_scr, vb_scr, s_scr, p_scr, l_scr, toep_scr, bias_scr = rest
    in_a = lax.broadcasted_iota(jnp.int32, (1, 128), 1) < NA_DH
    kc = kc_ref[...]
    vc = vc_ref[...]
    blk = NA_KH * NA_CB

    def relayout(row, carry):
        dst = pl.ds(pl.multiple_of(row * NA_CB, NA_CB), NA_CB)
        for cb in range(NA_NCB):
            src = pl.ds(pl.multiple_of(row * GRID_W + cb * NA_CB, NA_CB), NA_CB)
            kb_scr[cb, dst, :] = k_ref[src, :]
            vb_scr[cb, dst, :] = v_ref[src, :]
        return carry

    lax.fori_loop(0, ROWS, relayout, 0)
    p_scr[...] = jnp.zeros_like(p_scr)

    def window(r):
        if isinstance(r, int):
            rs = min(max(r - NA_KH // 2, 0), ROWS - NA_KH)
            return r * GRID_W, rs * NA_CB, r - rs
        rs = jnp.clip(r - NA_KH // 2, 0, ROWS - NA_KH)
        return pl.multiple_of(r * GRID_W, GRID_W), pl.multiple_of(rs * NA_CB, NA_CB), r - rs

    def key_window(ref, w0):
        return jnp.concatenate([ref[cb, pl.ds(w0, blk), :] for cb in range(NA_NCB)], axis=0)

    def live(group):
        b0, b1 = _live_key_blocks(group % (GRID_W // 8))
        return slice(8 * group, 8 * group + 8), slice(b0 * blk, b1 * blk)

    n_groups8 = 2 * GRID_W // 8
    n_rel_r = 2 * NA_KH - 1

    @pl.when(pl.program_id(1) == 0)
    def _():
        c_idx = lax.broadcasted_iota(jnp.int32, (GRID_W, 128), 0)
        kc_idx = lax.broadcasted_iota(jnp.int32, (GRID_W, 128), 1)
        cs = jnp.clip(c_idx - NA_KW // 2, 0, GRID_W - NA_KW)
        valid = (kc_idx >= cs) & (kc_idx < cs + NA_KW)
        for head in range(2):
            for rr in range(n_rel_r):
                row = jnp.broadcast_to(rpb_ref[0, head, rr:rr + 1, :], (GRID_W, 128))
                toep = pltpu.roll(row, 128 - (GRID_W - 1), 1, stride=1, stride_axis=0)
                toep_scr[rr, head * GRID_W:(head + 1) * GRID_W, :] = jnp.where(valid, toep, NEG)

        def per_offset(var, carry):
            for g in range(n_groups8):
                rows, _ = live(g)
                b0, b1 = _live_key_blocks(g % (GRID_W // 8))
                for i in range(NA_KH):
                    src = toep_scr[i - var + NA_KH - 1, rows, :]
                    for cb in range(b0, b1):
                        dst = cb * blk + i * NA_CB
                        bias_scr[var, rows, dst:dst + NA_CB] = src[:, cb * NA_CB:(cb + 1) * NA_CB]
            return carry

        lax.fori_loop(0, NA_KH, per_offset, 0)

    def scores(r, slot):
        q0, w0, var = window(r)
        q2 = _split_heads(q_ref[pl.ds(q0, GRID_W), :], in_a)
        s_loc = _dot_nt(q2, key_window(kb_scr, w0))
        for g in range(n_groups8):
            rows, lanes = live(g)
            s_scr[slot, rows, lanes] = s_loc[rows, lanes] + bias_scr[var, rows, lanes]
        s_scr[slot, :, NA_WIN:] = _dot_nt(q2, kc)

    def softmax(slot):
        for g in range(n_groups8):
            rows, lanes = live(g)
            s = jnp.concatenate([s_scr[slot, rows, lanes], s_scr[slot, rows, NA_WIN:]], axis=1)
            p = jnp.exp2(s - jnp.max(s, axis=-1, keepdims=True))
            l_scr[slot, rows, :] = jnp.broadcast_to(jnp.sum(p, axis=-1, keepdims=True), (8, 128))
            n_loc = lanes.stop - lanes.start
            p_scr[slot, rows, lanes] = p[:, :n_loc]
            p_scr[slot, rows, NA_WIN:] = p[:, n_loc:]

    def values(r, slot):
        q0, w0, _ = window(r)
        p = p_scr[slot].astype(BF16)
        v = jnp.concatenate([key_window(vb_scr, w0), vc], axis=0)
        o2 = _dot(p, v) / l_scr[slot]
        o_ref[pl.ds(q0, GRID_W), :] = jnp.where(in_a, o2[:GRID_W], o2[GRID_W:]).astype(o_ref.dtype)

    def stage_values(t):
        for u in range(NA_GROUP):
            values(NA_GROUP * t + u, NA_GROUP * (t % 2) + u)

    def stage_softmax(t):
        for u in range(NA_GROUP):
            softmax(NA_GROUP * (t % 2) + u)

    def stage_scores(t):
        for u in range(NA_GROUP):
            scores(NA_GROUP * t + u, NA_GROUP * (t % 2) + u)

    n_groups = ROWS // NA_GROUP
    stage_scores(0)
    stage_softmax(0)
    stage_scores(1)

    def step(t, carry):
        stage_values(t - 2)
        stage_softmax(t - 1)
        stage_scores(t)
        return carry

    lax.fori_loop(2, n_groups, step, 0)
    stage_values(n_groups - 2)
    stage_softmax(n_groups - 1)
    stage_values(n_groups - 1)

    if need_ctx:
        q2 = _split_heads(qc_ref[...], in_a)
        s = _dot_nt(q2, kc)
        p = jnp.exp2(s - jnp.max(s, axis=-1, keepdims=True))
        o2 = _dot(p.astype(BF16), vc) / jnp.sum(p, axis=-1, keepdims=True)
        oc_ref[...] = jnp.where(in_a, o2[:CTX], o2[CTX:]).astype(oc_ref.dtype)


def na_attention(qkv_x, qkv_c, rpb, *, need_ctx):
    def col(sec):
        return lambda p, b: (b, sec * NA_PAIRS + p)
    in_specs = [
        pl.BlockSpec((SEQ, 128), col(0)),
        pl.BlockSpec((SEQ, 128), col(1)),
        pl.BlockSpec((SEQ, 128), col(2)),
        pl.BlockSpec((CTX, 128), col(1)),
        pl.BlockSpec((CTX, 128), col(2)),
        pl.BlockSpec((1, 2, 2 * NA_KH, 128), lambda p, b: (p, 0, 0, 0)),
    ]
    args = [qkv_x, qkv_x, qkv_x, qkv_c, qkv_c, rpb]
    out_specs = [pl.BlockSpec((SEQ, 128), lambda p, b: (b, p))]
    out_shape = [jax.ShapeDtypeStruct((BATCH * SEQ, D), BF16)]
    if need_ctx:
        in_specs.append(pl.BlockSpec((CTX, 128), col(0)))
        args.append(qkv_c)
        out_specs.append(pl.BlockSpec((CTX, 128), lambda p, b: (b, p)))
        out_shape.append(jax.ShapeDtypeStruct((BATCH * CTX, D), BF16))
    out = pl.pallas_call(
        functools.partial(_na_attn_kernel, need_ctx=need_ctx),
        grid=(NA_PAIRS, BATCH),
        in_specs=in_specs,
        out_specs=out_specs,
        out_shape=out_shape,
        scratch_shapes=[
            pltpu.VMEM((NA_NCB, ROWS * NA_CB, 128), BF16),
            pltpu.VMEM((NA_NCB, ROWS * NA_CB, 128), BF16),
            pltpu.VMEM((2 * NA_GROUP, 2 * GRID_W, NA_WIN + CTX), F32),
            pltpu.VMEM((2 * NA_GROUP, 2 * GRID_W, NA_WIN + CTX), F32),
            pltpu.VMEM((2 * NA_GROUP, 2 * GRID_W, 128), F32),
            pltpu.VMEM((2 * NA_KH - 1, 2 * GRID_W, 128), F32),
            pltpu.VMEM((NA_KH, 2 * GRID_W, NA_WIN), F32),
        ],
        compiler_params=_params("parallel", "arbitrary"),
        name="na_attention",
    )(*args)
    return (out[0], out[1]) if need_ctx else (out[0], None)


def na_rpb_rows(rpb):
    lpad = GRID_W - NA_KW
    p = jnp.pad(rpb * LOG2E, ((0, 0), (0, 1), (lpad, 128 - lpad - (2 * NA_KW - 1))))
    return p.reshape(NA_PAIRS, 2, 2 * NA_KH, 128)


def _proj_res_kernel(a_ref, w_ref, b_ref, gate_ref, res_ref, o_ref):
    y = _dot(a_ref[...], w_ref[...]) + b_ref[...]
    o_ref[...] = res_ref[...] + gate_ref[0] * y


def proj_residual(a, w, b, gate, res, *, layer, tm, row_of):
    m, k = a.shape
    return pl.pallas_call(
        _proj_res_kernel,
        grid=(m // tm,),
        in_specs=[
            pl.BlockSpec((tm, k), lambda i: (i, 0)),
            pl.BlockSpec((None, k, D), lambda i: (layer, 0, 0)),
            _vec_spec(D), _mod_spec(row_of),
            pl.BlockSpec((tm, D), lambda i: (i, 0)),
        ],
        out_specs=pl.BlockSpec((tm, D), lambda i: (i, 0)),
        out_shape=jax.ShapeDtypeStruct((m, D), F32),
        compiler_params=_params("parallel"),
        name="proj_residual",
    )(a, w, b, gate, res)


def _ffn_kernel(xp_ref, x_ref, xn_ref, sh_ref, sc_ref, gate_ref, gain_ref, wup_ref, cw_ref, cb_ref, wd_ref,
                o_ref, h_scr, *, tm, tiles_per_seq):
    hl = 16
    gain, shift, scale = gain_ref[...], sh_ref[0], sc_ref[0]
    t = pl.program_id(0) % tiles_per_seq
    hp = _mod_norm(xp_ref[...], gain, shift, scale)
    hn = _mod_norm(xn_ref[...], gain, shift, scale)
    h_scr[0:hl] = jnp.where(t == 0, 0.0, hp).astype(BF16)
    h_scr[hl:hl + tm] = _mod_norm(x_ref[...], gain, shift, scale).astype(BF16)
    h_scr[hl + tm:] = jnp.where(t == tiles_per_seq - 1, 0.0, hn).astype(BF16)

    acc = None
    for c0, c1 in zip(FFN_SPLITS[:-1], FFN_SPLITS[1:]):
        u = _dot(h_scr[...], wup_ref[:, c0:c1])
        v = _dot(h_scr[hl:hl + tm], wup_ref[:, FFN_H + c0:FFN_H + c1])
        uc = (cw_ref[0:1, c0:c1] * u[hl - 1:hl - 1 + tm] + cw_ref[1:2, c0:c1] * u[hl:hl + tm]
              + cw_ref[2:3, c0:c1] * u[hl + 1:hl + 1 + tm] + cb_ref[:, c0:c1])
        g = (jax.nn.gelu(uc, approximate=True) * v).astype(BF16)
        part = _dot(g, wd_ref[c0:c1, :])
        acc = part if acc is None else acc + part
    o_ref[...] = x_ref[...] + gate_ref[0] * acc


def conv_ffn(x, sh, sc, gate, gain, w_up, cw, cb, wd, *, layer, tm, seq, row_of):
    m = x.shape[0]
    hb = tm // 16
    last_hb = m // 16 - 1

    def resident(shape):
        return pl.BlockSpec((None,) + shape, lambda i: (layer, 0, 0), pipeline_mode=pl.Buffered(1))

    return pl.pallas_call(
        functools.partial(_ffn_kernel, tm=tm, tiles_per_seq=seq // tm),
        grid=(m // tm,),
        in_specs=[
            pl.BlockSpec((16, D), lambda i: (jnp.maximum(i * hb - 1, 0), 0)),
            pl.BlockSpec((tm, D), lambda i: (i, 0)),
            pl.BlockSpec((16, D), lambda i: (jnp.minimum((i + 1) * hb, last_hb), 0)),
            _mod_spec(row_of), _mod_spec(row_of), _mod_spec(row_of), _vec_spec(D),
            resident((D, 2 * FFN_H)), resident((3, FFN_H)), resident((1, FFN_H)), resident((FFN_H, D)),
        ],
        out_specs=pl.BlockSpec((tm, D), lambda i: (i, 0)),
        out_shape=jax.ShapeDtypeStruct((m, D), F32),
        scratch_shapes=[pltpu.VMEM((tm + 32, D), BF16)],
        compiler_params=_params("parallel"),
        name="conv_ffn",
    )(x, x, x, sh, sc, gate, gain, w_up, cw, cb, wd)


def _ret_proj_kernel(x_ref, sh_ref, sc_ref, gain_ref, w_ref, cos_ref, sin_ref, o_ref, *, rope):
    h = _mod_norm(x_ref[...], gain_ref[...], sh_ref[0], sc_ref[0]).astype(BF16)
    qk_width = 2 * RET_HEADS * RET_DK
    v_end = qk_width + RET_HEADS * RET_DV
    k_scale = RET_DK ** -0.5
    for hh in range(2 * RET_HEADS):
        y = _dot(h, w_ref[:, hh * RET_DK:(hh + 1) * RET_DK])
        s = k_scale if hh >= RET_HEADS else 1.0
        x1, x2 = y[:, :128], y[:, 128:]
        if rope:
            cos, sin = cos_ref[...], sin_ref[...]
            x1, x2 = x1 * cos - x2 * sin, x1 * sin + x2 * cos
        o_ref[:, hh * RET_DK: hh * RET_DK + 128] = (x1 * s).astype(o_ref.dtype)
        o_ref[:, hh * RET_DK + 128: (hh + 1) * RET_DK] = (x2 * s).astype(o_ref.dtype)
    o_ref[:, qk_width:v_end] = _dot(h, w_ref[:, qk_width:v_end]).astype(o_ref.dtype)
    g = _dot(h, w_ref[:, v_end:])
    o_ref[:, v_end:] = (g * jax.nn.sigmoid(g)).astype(o_ref.dtype)


def ret_proj(x, sh, sc, gain, w, cos, sin, *, tm, seq, row_of, rope):
    m = x.shape[0]
    n = w.shape[1]
    tps = seq // tm
    return pl.pallas_call(
        functools.partial(_ret_proj_kernel, rope=rope),
        grid=(m // tm,),
        in_specs=[
            pl.BlockSpec((tm, D), lambda i: (i, 0)),
            _mod_spec(row_of), _mod_spec(row_of), _vec_spec(D),
            pl.BlockSpec((D, n), lambda i: (0, 0), pipeline_mode=pl.Buffered(1)),
            pl.BlockSpec((tm, 128), lambda i: (i % tps, 0)),
            pl.BlockSpec((tm, 128), lambda i: (i % tps, 0)),
        ],
        out_specs=pl.BlockSpec((tm, n), lambda i: (i, 0)),
        out_shape=jax.ShapeDtypeStruct((m, n), BF16),
        compiler_params=_params("parallel"),
        name="ret_proj",
    )(x, sh, sc, gain, w, cos, sin)


def _log_sigmoid(x):
    return jnp.minimum(x, 0.0) - jnp.log1p(jnp.exp(-jnp.abs(x)))


def _ret_scan_kernel(qx_ref, kx_ref, vx_ref, qc_ref, kc_ref, vc_ref, dl_ref, ox_ref, oc_ref,
                     sf_scr, sb_scr, ax_scr, mask_scr):
    C = RET_C
    n = SEQ // C
    pos = lax.broadcasted_iota(jnp.int32, (C, 128), 0).astype(F32)
    ii = lax.broadcasted_iota(jnp.int32, (C, C), 0).astype(F32)
    jj = lax.broadcasted_iota(jnp.int32, (C, C), 1).astype(F32)

    def lanes(x, width):
        return jnp.concatenate([x] * (width // 128), axis=1)

    def decays(direction):
        lg = jnp.broadcast_to(_log_sigmoid(dl_ref[direction, 0])[0:1, :], (C, 128))
        diff = ii - jj if direction == 0 else jj - ii
        p = pos if direction == 0 else (C - 1.0) - pos
        intra = jnp.where(diff >= 0, jnp.exp(jnp.maximum(diff, 0.0) * lanes(lg, C)), 0.0)
        qd = jnp.exp((p + 1.0) * lg)
        kd = jnp.exp((C - 1.0 - p) * lg)
        cd = jnp.exp(C * lg)[0:1]
        return intra, qd, kd, cd

    intra_f, qd_f, kd_f, cd_f = decays(0)
    intra_b, qd_b, kd_b, cd_b = decays(1)
    mask_scr[...] = intra_f + intra_b

    def group_norm(o):
        mu = jnp.mean(o, axis=-1, keepdims=True)
        oc = o - mu
        return oc * lax.rsqrt(jnp.mean(oc * oc, axis=-1, keepdims=True) + EPS)

    def intra_term(q, k, v):
        return _dot((_dot_nt(q, k) * mask_scr[...]).astype(BF16), v)

    def cross_term(q, s_scr, qd):
        return lanes(qd, RET_DV) * _dot(q, s_scr[...].astype(BF16))

    def state_term(k, v, kd):
        return _dot_tn((k.astype(F32) * lanes(kd, RET_DK)).astype(BF16), v)

    def advance(s_scr, k, v, kd, cd):
        s_scr[...] = s_scr[...] * lanes(cd, RET_DV) + state_term(k, v, kd)

    q, k, v = qc_ref[...], kc_ref[...], vc_ref[...]
    oc_ref[...] = group_norm(intra_term(q, k, v)).astype(oc_ref.dtype)
    sf_scr[...] = state_term(k, v, kd_f)
    sb_scr[...] = state_term(k, v, kd_b)

    def visit(t, finish):
        rf = pl.multiple_of(t * C, C)
        rb = pl.multiple_of((n - 1 - t) * C, C)
        q, k, v = qx_ref[pl.ds(rf, C), :], kx_ref[pl.ds(rf, C), :], vx_ref[pl.ds(rf, C), :]
        o = intra_term(q, k, v) + cross_term(q, sf_scr, qd_f)
        if finish:
            ox_ref[pl.ds(rf, C), :] = group_norm(ax_scr[pl.ds(rf, C), :] + o).astype(ox_ref.dtype)
        else:
            ax_scr[pl.ds(rf, C), :] = o
        advance(sf_scr, k, v, kd_f, cd_f)
        q, k, v = qx_ref[pl.ds(rb, C), :], kx_ref[pl.ds(rb, C), :], vx_ref[pl.ds(rb, C), :]
        o = cross_term(q, sb_scr, qd_b)
        if finish:
            ox_ref[pl.ds(rb, C), :] = group_norm(ax_scr[pl.ds(rb, C), :] + o).astype(ox_ref.dtype)
        else:
            ax_scr[pl.ds(rb, C), :] = o
        advance(sb_scr, k, v, kd_b, cd_b)

    def first_visits(t, carry):
        visit(t, False)
        return carry

    def second_visits(t, carry):
        visit(t, True)
        return carry

    lax.fori_loop(0, n // 2, first_visits, 0)
    lax.fori_loop(n // 2, n, second_visits, 0)


def ret_scan(px, pc, dl_tiles):
    kb = RET_HEADS
    vb = 2 * RET_HEADS * RET_DK // RET_DV
    return pl.pallas_call(
        _ret_scan_kernel,
        grid=(BATCH, RET_HEADS),
        in_specs=[
            pl.BlockSpec((SEQ, RET_DK), lambda b, h: (b, h)),
            pl.BlockSpec((SEQ, RET_DK), lambda b, h: (b, kb + h)),
            pl.BlockSpec((SEQ, RET_DV), lambda b, h: (b, vb + h)),
            pl.BlockSpec((CTX, RET_DK), lambda b, h: (b, h)),
            pl.BlockSpec((CTX, RET_DK), lambda b, h: (b, kb + h)),
            pl.BlockSpec((CTX, RET_DV), lambda b, h: (b, vb + h)),
            pl.BlockSpec((2, 1, 8, 128), lambda b, h: (0, h, 0, 0)),
        ],
        out_specs=[
            pl.BlockSpec((SEQ, RET_DV), lambda b, h: (b, h)),
            pl.BlockSpec((CTX, RET_DV), lambda b, h: (b, h)),
        ],
        out_shape=[
            jax.ShapeDtypeStruct((BATCH * SEQ, RET_HEADS * RET_DV), BF16),
            jax.ShapeDtypeStruct((BATCH * CTX, RET_HEADS * RET_DV), BF16),
        ],
        scratch_shapes=[
            pltpu.VMEM((RET_DK, RET_DV), F32),
            pltpu.VMEM((RET_DK, RET_DV), F32),
            pltpu.VMEM((SEQ, RET_DV), F32),
            pltpu.VMEM((RET_C, RET_C), F32),
        ],
        compiler_params=_params("parallel", "parallel"),
        name="ret_scan",
    )(px, px, px, pc, pc, pc, dl_tiles)


def _ret_out_kernel(o_ref, g_ref, gn_ref, w_ref, gate_ref, res_ref, out_ref):
    a = (o_ref[...].astype(F32) * gn_ref[...] * g_ref[...].astype(F32)).astype(BF16)
    out_ref[...] = res_ref[...] + gate_ref[0] * _dot(a, w_ref[...])


def ret_out(o, proj, gn, w, gate, res, *, tm, row_of):
    m = o.shape[0]
    dv = RET_HEADS * RET_DV
    return pl.pallas_call(
        _ret_out_kernel,
        grid=(m // tm,),
        in_specs=[
            pl.BlockSpec((tm, dv), lambda i: (i, 0)),
            pl.BlockSpec((tm, dv), lambda i: (i, 2)),
            _vec_spec(dv),
            pl.BlockSpec((dv, D), lambda i: (0, 0)),
            _mod_spec(row_of),
            pl.BlockSpec((tm, D), lambda i: (i, 0)),
        ],
        out_specs=pl.BlockSpec((tm, D), lambda i: (i, 0)),
        out_shape=jax.ShapeDtypeStruct((m, D), F32),
        compiler_params=_params("parallel"),
        name="ret_out",
    )(o, proj, gn, w, gate, res)


def _pw1_glu_kernel(x_ref, sh_ref, sc_ref, gain_ref, w_ref, b_ref, o_ref):
    h = _mod_norm(x_ref[...], gain_ref[...], sh_ref[0], sc_ref[0]).astype(BF16)
    y = _dot(h, w_ref[...]) + b_ref[...]
    o_ref[...] = y[:, :D] * jax.nn.sigmoid(y[:, D:])


def pw1_glu(x, sh, sc, gain, w, b, *, tm, row_of):
    m = x.shape[0]
    return pl.pallas_call(
        _pw1_glu_kernel,
        grid=(m // tm,),
        in_specs=[
            pl.BlockSpec((tm, D), lambda i: (i, 0)),
            _mod_spec(row_of), _mod_spec(row_of), _vec_spec(D),
            pl.BlockSpec((D, 2 * D), lambda i: (0, 0)),
            _vec_spec(2 * D),
        ],
        out_specs=pl.BlockSpec((tm, D), lambda i: (i, 0)),
        out_shape=jax.ShapeDtypeStruct((m, D), F32),
        compiler_params=_params("parallel"),
        name="pw1_glu",
    )(x, sh, sc, gain, w, b)


def _dwconv_kernel(up_ref, u_ref, un_ref, cw_ref, cb_ref, lg_ref, lb_ref, w_ref, b_ref, gate_ref, res_ref,
                   o_ref, ue_scr, y_scr, *, tm, tiles_per_seq):
    t = pl.program_id(0) % tiles_per_seq
    hl = CONV_HALO
    ue_scr[0:hl] = jnp.where(t == 0, 0.0, up_ref[...])
    ue_scr[hl:hl + tm] = u_ref[...]
    ue_scr[hl + tm:] = jnp.where(t == tiles_per_seq - 1, 0.0, un_ref[...])

    base = hl - CONV_W // 2
    win = CONV_RB + 2 * hl
    for cblk in range(D // 128):
        cols = slice(cblk * 128, (cblk + 1) * 128)

        def row_block(rb, carry, cols=cols):
            r0 = pl.multiple_of(rb * CONV_RB, CONV_RB)
            window = ue_scr[pl.ds(r0, win), cols]
            acc = jnp.zeros((CONV_RB, 128), F32)
            for b in range(8):
                shifted = window if b == 0 else pltpu.roll(window, win - b, axis=0)
                for a in range((base + CONV_W - 1) // 8 + 1):
                    k = 8 * a + b - base
                    if 0 <= k < CONV_W:
                        acc = acc + cw_ref[k:k + 1, cols] * shifted[8 * a:8 * a + CONV_RB]
            y_scr[pl.ds(r0, CONV_RB), cols] = acc + cb_ref[:, cols]
            return carry

        lax.fori_loop(0, tm // CONV_RB, row_block, 0)

    y = y_scr[...]
    mu = jnp.mean(y, axis=-1, keepdims=True)
    yc = y - mu
    var = jnp.mean(yc * yc, axis=-1, keepdims=True)
    z = yc * lax.rsqrt(var + EPS) * lg_ref[...] + lb_ref[...]
    z = (z * jax.nn.sigmoid(z)).astype(BF16)
    o_ref[...] = res_ref[...] + gate_ref[0] * (_dot(z, w_ref[...]) + b_ref[...])


def dwconv_out(u, cw, cb, lg, lb, w, b, gate, res, *, tm, seq, row_of):
    m = u.shape[0]
    hb = tm // CONV_HALO
    last_hb = m // CONV_HALO - 1
    return pl.pallas_call(
        functools.partial(_dwconv_kernel, tm=tm, tiles_per_seq=seq // tm),
        grid=(m // tm,),
        in_specs=[
            pl.BlockSpec((CONV_HALO, D), lambda i: (jnp.maximum(i * hb - 1, 0), 0)),
            pl.BlockSpec((tm, D), lambda i: (i, 0)),
            pl.BlockSpec((CONV_HALO, D), lambda i: (jnp.minimum((i + 1) * hb, last_hb), 0)),
            pl.BlockSpec((CONV_W, D), lambda i: (0, 0)),
            _vec_spec(D), _vec_spec(D), _vec_spec(D),
            pl.BlockSpec((D, D), lambda i: (0, 0)),
            _vec_spec(D), _mod_spec(row_of),
            pl.BlockSpec((tm, D), lambda i: (i, 0)),
        ],
        out_specs=pl.BlockSpec((tm, D), lambda i: (i, 0)),
        out_shape=jax.ShapeDtypeStruct((m, D), F32),
        scratch_shapes=[pltpu.VMEM((tm + 2 * CONV_HALO, D), F32), pltpu.VMEM((tm, D), F32)],
        compiler_params=_params("parallel"),
        name="dwconv_out",
    )(u, u, u, cw, cb, lg, lb, w, b, gate, res)


def _rope_tables():
    t = np.arange(SEQ)
    row = (t // GRID_W).astype(np.float32)
    col = (t % GRID_W).astype(np.float32)
    n_freq = RET_DK // 4
    inv_freq = jnp.power(ROPE_BASE, -jnp.arange(n_freq, dtype=F32) / n_freq)
    ang = jnp.concatenate([row[:, None] * inv_freq, col[:, None] * inv_freq], axis=-1)
    return jnp.cos(ang), jnp.sin(ang)


def kernel(x, c, ctx, c_ctx, ada_w, ada_b, norm_mix, norm_ffn, na_w_qkv, na_w_o, na_q_gain, na_k_gain, na_rpb, ret_w_q, ret_w_k, ret_w_v, ret_w_g, ret_w_o, ret_gn_gain, ret_decay_logit, cv_w_pw1, cv_b_pw1, cv_w_dw, cv_b_dw, cv_ln_g, cv_ln_b, cv_w_pw2, cv_b_pw2, ffn_w_up, ffn_w_dw, ffn_b_dw, ffn_w_down):
    xs = x.reshape(BATCH * SEQ, D)
    cs = ctx.reshape(BATCH * CTX, D)
    tm_x, tm_c = 512, CTX
    tiles_per_batch = SEQ // tm_x

    def x_row(i):
        return i // tiles_per_batch

    def c_row(i):
        return CTX_ROW

    c_rows = jnp.zeros((MOD_ROWS, D), F32).at[:BATCH].set(c).at[CTX_ROW].set(c_ctx)
    mods = ada_table(c_rows, ada_w, ada_b).reshape(DEPTH, MOD_ROWS, 6, 1, D)
    zero_bias = jnp.zeros((1, D), F32)
    cos, sin = _rope_tables()
    w_qkv, w_o_na = na_w_qkv.astype(BF16), na_w_o.astype(BF16)
    w_up, w_down = ffn_w_up.astype(BF16), ffn_w_down.astype(BF16)
    ffn_cb = ffn_b_dw.reshape(DEPTH, 1, FFN_H)

    for i in range(DEPTH):
        need_ctx = i < DEPTH - 1
        mod = [mods[i, :, k] for k in range(6)]
        g_mix = norm_mix[i].reshape(1, D)
        g_ffn = norm_ffn[i].reshape(1, D)
        kind, j = i % 3, i // 3
        if kind == 0:
            qg = (jnp.tile(na_q_gain[j], NA_HEADS) * (NA_SCALE * LOG2E)).reshape(1, D)
            kg = jnp.tile(na_k_gain[j], NA_HEADS).reshape(1, D)
            qkv_x = na_qkv(xs, mod[0], mod[1], g_mix, w_qkv, qg, kg, layer=j, tm=tm_x, row_of=x_row)
            qkv_c = na_qkv(cs, mod[0], mod[1], g_mix, w_qkv, qg, kg, layer=j, tm=tm_c, row_of=c_row)
            ox, oc = na_attention(qkv_x, qkv_c, na_rpb_rows(na_rpb[j]), need_ctx=need_ctx)
            xs = proj_residual(ox, w_o_na, zero_bias, mod[2], xs, layer=j, tm=tm_x, row_of=x_row)
            if need_ctx:
                cs = proj_residual(oc, w_o_na, zero_bias, mod[2], cs, layer=j, tm=tm_c, row_of=c_row)
        elif kind == 1:
            w = jnp.concatenate([ret_w_q[j], ret_w_k[j], ret_w_v[j], ret_w_g[j]], axis=1).astype(BF16)
            px = ret_proj(xs, mod[0], mod[1], g_mix, w, cos, sin, tm=tm_x, seq=SEQ, row_of=x_row, rope=True)
            pc = ret_proj(cs, mod[0], mod[1], g_mix, w, cos[:CTX], sin[:CTX], tm=tm_c, seq=CTX, row_of=c_row,
                          rope=False)
            dl = jnp.broadcast_to(ret_decay_logit[j][:, :, None, None], (2, RET_HEADS, 8, 128))
            ox, oc = ret_scan(px, pc, dl)
            w_o = ret_w_o[j].astype(BF16)
            gn = ret_gn_gain[j].reshape(1, RET_HEADS * RET_DV)
            xs = ret_out(ox, px, gn, w_o, mod[2], xs, tm=tm_x, row_of=x_row)
            if need_ctx:
                cs = ret_out(oc, pc, gn, w_o, mod[2], cs, tm=tm_c, row_of=c_row)
        else:
            w1 = cv_w_pw1[j].astype(BF16)
            b1 = cv_b_pw1[j].reshape(1, 2 * D)
            w2 = cv_w_pw2[j].astype(BF16)
            conv_args = (cv_w_dw[j], cv_b_dw[j].reshape(1, D), cv_ln_g[j].reshape(1, D), cv_ln_b[j].reshape(1, D),
                         w2, cv_b_pw2[j].reshape(1, D))
            ux = pw1_glu(xs, mod[0], mod[1], g_mix, w1, b1, tm=tm_x, row_of=x_row)
            xs = dwconv_out(ux, *conv_args, mod[2], xs, tm=tm_x, seq=SEQ, row_of=x_row)
            if need_ctx:
                uc = pw1_glu(cs, mod[0], mod[1], g_mix, w1, b1, tm=tm_c, row_of=c_row)
                cs = dwconv_out(uc, *conv_args, mod[2], cs, tm=tm_c, seq=CTX, row_of=c_row)

        ffn_args = (mod[3], mod[4], mod[5], g_ffn, w_up, ffn_w_dw, ffn_cb, w_down)
        xs = conv_ffn(xs, *ffn_args, layer=i, tm=tm_x, seq=SEQ, row_of=x_row)
        if need_ctx:
            cs = conv_ffn(cs, *ffn_args, layer=i, tm=tm_c, seq=CTX, row_of=c_row)

    return xs.reshape(BATCH, SEQ, D)
```

```python
import functools

import numpy as np
import jax
import jax.numpy as jnp
from jax import lax
from jax.experimental import pallas as pl
from jax.experimental.pallas import tpu as pltpu

D = 1024
BATCH = 4
SEQ = 4096
DEPTH = 4
CTX = 256
GRID_W = 64
ROWS = SEQ // GRID_W
EPS = 1e-6

NA_HEADS = 16
NA_DH = 64
NA_KH = 8
NA_KW = 16
NA_SCALE = NA_DH ** -0.5
NA_PAIRS = NA_HEADS // 2
NA_WIN = NA_KH * GRID_W
NA_GROUP = 2
NA_CB = NA_KW
NA_NCB = GRID_W // NA_CB

RET_HEADS = 4
RET_DK = 256
RET_DV = 512
RET_C = 256
assert CTX == RET_C and (SEQ // RET_C) % 2 == 0
ROPE_BASE = 10000.0

CONV_W = 31
CONV_HALO = 16
CONV_RB = 128
FFN_H = 2816
FFN_SPLITS = (0, 1536, FFN_H)

MOD_ROWS = 8
CTX_ROW = BATCH
NEG = -1e30
LOG2E = 1.4426950408889634

F32 = jnp.float32
BF16 = jnp.bfloat16

VMEM_LIMIT = 52 * 1024 * 1024


def _params(*sem):
    return pltpu.CompilerParams(dimension_semantics=sem, vmem_limit_bytes=VMEM_LIMIT)


def _mod_norm(x, gain, shift, scale):
    ms = jnp.mean(x * x, axis=-1, keepdims=True)
    return x * lax.rsqrt(ms + EPS) * gain * (1.0 + scale) + shift


def _dot(a, b):
    return jnp.dot(a, b, preferred_element_type=F32)


def _dot_nt(a, b):
    return lax.dot_general(a, b, (((1,), (1,)), ((), ())), preferred_element_type=F32)


def _dot_tn(a, b):
    return lax.dot_general(a, b, (((0,), (0,)), ((), ())), preferred_element_type=F32)


def _ada_kernel(c_ref, w_ref, b_ref, o_ref):
    c = c_ref[...]
    s = (c * jax.nn.sigmoid(c)).astype(BF16)
    o_ref[0] = _dot(s, w_ref[0].astype(BF16)) + b_ref[0]


def ada_table(c_rows, ada_w, ada_b):
    tn = 1536
    return pl.pallas_call(
        _ada_kernel,
        grid=(DEPTH, 6 * D // tn),
        in_specs=[
            pl.BlockSpec((MOD_ROWS, D), lambda l, j: (0, 0)),
            pl.BlockSpec((1, D, tn), lambda l, j: (l, 0, j)),
            pl.BlockSpec((1, 1, tn), lambda l, j: (l, 0, j)),
        ],
        out_specs=pl.BlockSpec((1, MOD_ROWS, tn), lambda l, j: (l, 0, j)),
        out_shape=jax.ShapeDtypeStruct((DEPTH, MOD_ROWS, 6 * D), F32),
        compiler_params=_params("parallel", "parallel"),
        name="ada_table",
    )(c_rows, ada_w, ada_b.reshape(DEPTH, 1, 6 * D))


def _mod_spec(row_of):
    return pl.BlockSpec((1, 1, D), lambda i, *_: (row_of(i), 0, 0))


def _vec_spec(n):
    return pl.BlockSpec((1, n), lambda i, *_: (0, 0))


def _na_qkv_kernel(x_ref, sh_ref, sc_ref, gain_ref, w_ref, qg_ref, kg_ref, o_ref):
    h = _mod_norm(x_ref[...], gain_ref[...], sh_ref[0], sc_ref[0]).astype(BF16)
    gw = 256
    r = lax.broadcasted_iota(jnp.int32, (gw, gw), 0) // NA_DH
    c = lax.broadcasted_iota(jnp.int32, (gw, gw), 1) // NA_DH
    seg = jnp.where(r == c, 1.0, 0.0).astype(BF16)
    for sec, g_ref in ((0, qg_ref), (1, kg_ref)):
        y = _dot(h, w_ref[:, sec * D:(sec + 1) * D])
        for g in range(D // gw):
            yg = y[:, g * gw:(g + 1) * gw]
            ss = _dot((yg * yg).astype(BF16), seg)
            yn = yg * lax.rsqrt(ss * (1.0 / NA_DH) + EPS) * g_ref[:, g * gw:(g + 1) * gw]
            o_ref[:, sec * D + g * gw: sec * D + (g + 1) * gw] = yn.astype(o_ref.dtype)
    o_ref[:, 2 * D:] = _dot(h, w_ref[:, 2 * D:]).astype(o_ref.dtype)


def na_qkv(x, sh, sc, gain, w, qg, kg, *, layer, tm, row_of):
    m = x.shape[0]
    return pl.pallas_call(
        _na_qkv_kernel,
        grid=(m // tm,),
        in_specs=[
            pl.BlockSpec((tm, D), lambda i: (i, 0)),
            _mod_spec(row_of), _mod_spec(row_of), _vec_spec(D),
            pl.BlockSpec((None, D, 3 * D), lambda i: (layer, 0, 0)),
            _vec_spec(D), _vec_spec(D),
        ],
        out_specs=pl.BlockSpec((tm, 3 * D), lambda i: (i, 0)),
        out_shape=jax.ShapeDtypeStruct((m, 3 * D), BF16),
        compiler_params=_params("parallel"),
        name="na_qkv",
    )(x, sh, sc, gain, w, qg, kg)


def _split_heads(q, in_a):
    zero = jnp.zeros_like(q)
    return jnp.concatenate([jnp.where(in_a, q, zero), jnp.where(in_a, zero, q)], axis=0)


def _live_key_blocks(group):
    lo = min(max(8 * group - NA_KW // 2, 0), GRID_W - NA_KW)
    hi = min(max(8 * group + 7 - NA_KW // 2, 0), GRID_W - NA_KW) + NA_KW - 1
    return lo // NA_CB, hi // NA_CB + 1


def _na_attn_kernel(q_ref, k_ref, v_ref, kc_ref, vc_ref, rpb_ref, *rest, need_ctx):
    if need_ctx:
        qc_ref, o_ref, oc_ref, kb_scr, vb_scr, s_scr, p_scr, l_scr, toep_scr, bias_scr = rest
    else:
        o_ref, kb_scr, vb_scr, s_scr, p_scr, l_scr, toep_scr, bias_scr = rest
    in_a = lax.broadcasted_iota(jnp.int32, (1, 128), 1) < NA_DH
    kc = kc_ref[...]
    vc = vc_ref[...]
    blk = NA_KH * NA_CB

    def relayout(row, carry):
        dst = pl.ds(pl.multiple_of(row * NA_CB, NA_CB), NA_CB)
        for cb in range(NA_NCB):
            src = pl.ds(pl.multiple_of(row * GRID_W + cb * NA_CB, NA_CB), NA_CB)
            kb_scr[cb, dst, :] = k_ref[src, :]
            vb_scr[cb, dst, :] = v_ref[src, :]
        return carry

    lax.fori_loop(0, ROWS, relayout, 0)
    p_scr[...] = jnp.zeros_like(p_scr)

    def window(r):
        if isinstance(r, int):
            rs = min(max(r - NA_KH // 2, 0), ROWS - NA_KH)
            return r * GRID_W, rs * NA_CB, r - rs
        rs = jnp.clip(r - NA_KH // 2, 0, ROWS - NA_KH)
        return pl.multiple_of(r * GRID_W, GRID_W), pl.multiple_of(rs * NA_CB, NA_CB), r - rs

    def key_window(ref, w0):
        return jnp.concatenate([ref[cb, pl.ds(w0, blk), :] for cb in range(NA_NCB)], axis=0)

    def live(group):
        b0, b1 = _live_key_blocks(group % (GRID_W // 8))
        return slice(8 * group, 8 * group + 8), slice(b0 * blk, b1 * blk)

    n_groups8 = 2 * GRID_W // 8
    n_rel_r = 2 * NA_KH - 1

    @pl.when(pl.program_id(1) == 0)
    def _():
        c_idx = lax.broadcasted_iota(jnp.int32, (GRID_W, 128), 0)
        kc_idx = lax.broadcasted_iota(jnp.int32, (GRID_W, 128), 1)
        cs = jnp.clip(c_idx - NA_KW // 2, 0, GRID_W - NA_KW)
        valid = (kc_idx >= cs) & (kc_idx < cs + NA_KW)
        for head in range(2):
            for rr in range(n_rel_r):
                row = jnp.broadcast_to(rpb_ref[0, head, rr:rr + 1, :], (GRID_W, 128))
                toep = pltpu.roll(row, 128 - (GRID_W - 1), 1, stride=1, stride_axis=0)
                toep_scr[rr, head * GRID_W:(head + 1) * GRID_W, :] = jnp.where(valid, toep, NEG)

        def per_offset(var, carry):
            for g in range(n_groups8):
                rows, _ = live(g)
                b0, b1 = _live_key_blocks(g % (GRID_W // 8))
                for i in range(NA_KH):
                    src = toep_scr[i - var + NA_KH - 1, rows, :]
                    for cb in range(b0, b1):
                        dst = cb * blk + i * NA_CB
                        bias_scr[var, rows, dst:dst + NA_CB] = src[:, cb * NA_CB:(cb + 1) * NA_CB]
            return carry

        lax.fori_loop(0, NA_KH, per_offset, 0)

    def scores(r, slot):
        q0, w0, var = window(r)
        q2 = _split_heads(q_ref[pl.ds(q0, GRID_W), :], in_a)
        s_loc = _dot_nt(q2, key_window(kb_scr, w0))
        for g in range(n_groups8):
            rows, lanes = live(g)
            s_scr[slot, rows, lanes] = s_loc[rows, lanes] + bias_scr[var, rows, lanes]
        s_scr[slot, :, NA_WIN:] = _dot_nt(q2, kc)

    def softmax(slot):
        for g in range(n_groups8):
            rows, lanes = live(g)
            s = jnp.concatenate([s_scr[slot, rows, lanes], s_scr[slot, rows, NA_WIN:]], axis=1)
            p = jnp.exp2(s - jnp.max(s, axis=-1, keepdims=True))
            l_scr[slot, rows, :] = jnp.broadcast_to(jnp.sum(p, axis=-1, keepdims=True), (8, 128))
            n_loc = lanes.stop - lanes.start
            p_scr[slot, rows, lanes] = p[:, :n_loc]
            p_scr[slot, rows, NA_WIN:] = p[:, n_loc:]

    def values(r, slot):
        q0, w0, _ = window(r)
        p = p_scr[slot].astype(BF16)
        v = jnp.concatenate([key_window(vb_scr, w0), vc], axis=0)
        o2 = _dot(p, v) / l_scr[slot]
        o_ref[pl.ds(q0, GRID_W), :] = jnp.where(in_a, o2[:GRID_W], o2[GRID_W:]).astype(o_ref.dtype)

    def stage_values(t):
        for u in range(NA_GROUP):
            values(NA_GROUP * t + u, NA_GROUP * (t % 2) + u)

    def stage_softmax(t):
        for u in range(NA_GROUP):
            softmax(NA_GROUP * (t % 2) + u)

    def stage_scores(t):
        for u in range(NA_GROUP):
            scores(NA_GROUP * t + u, NA_GROUP * (t % 2) + u)

    n_groups = ROWS // NA_GROUP
    stage_scores(0)
    stage_softmax(0)
    stage_scores(1)

    def step(t, carry):
        stage_values(t - 2)
        stage_softmax(t - 1)
        stage_scores(t)
        return carry

    lax.fori_loop(2, n_groups, step, 0)
    stage_values(n_groups - 2)
    stage_softmax(n_groups - 1)
    stage_values(n_groups - 1)

    if need_ctx:
        q2 = _split_heads(qc_ref[...], in_a)
        s = _dot_nt(q2, kc)
        p = jnp.exp2(s - jnp.max(s, axis=-1, keepdims=True))
        o2 = _dot(p.astype(BF16), vc) / jnp.sum(p, axis=-1, keepdims=True)
        oc_ref[...] = jnp.where(in_a, o2[:CTX], o2[CTX:]).astype(oc_ref.dtype)


def na_attention(qkv_x, qkv_c, rpb, *, need_ctx):
    def col(sec):
        return lambda p, b: (b, sec * NA_PAIRS + p)
    in_specs = [
        pl.BlockSpec((SEQ, 128), col(0)),
        pl.BlockSpec((SEQ, 128), col(1)),
        pl.BlockSpec((SEQ, 128), col(2)),
        pl.BlockSpec((CTX, 128), col(1)),
        pl.BlockSpec((CTX, 128), col(2)),
        pl.BlockSpec((1, 2, 2 * NA_KH, 128), lambda p, b: (p, 0, 0, 0)),
    ]
    args = [qkv_x, qkv_x, qkv_x, qkv_c, qkv_c, rpb]
    out_specs = [pl.BlockSpec((SEQ, 128), lambda p, b: (b, p))]
    out_shape = [jax.ShapeDtypeStruct((BATCH * SEQ, D), BF16)]
    if need_ctx:
        in_specs.append(pl.BlockSpec((CTX, 128), col(0)))
        args.append(qkv_c)
        out_specs.append(pl.BlockSpec((CTX, 128), lambda p, b: (b, p)))
        out_shape.append(jax.ShapeDtypeStruct((BATCH * CTX, D), BF16))
    out = pl.pallas_call(
        functools.partial(_na_attn_kernel, need_ctx=need_ctx),
        grid=(NA_PAIRS, BATCH),
        in_specs=in_specs,
        out_specs=out_specs,
        out_shape=out_shape,
        scratch_shapes=[
            pltpu.VMEM((NA_NCB, ROWS * NA_CB, 128), BF16),
            pltpu.VMEM((NA_NCB, ROWS * NA_CB, 128), BF16),
            pltpu.VMEM((2 * NA_GROUP, 2 * GRID_W, NA_WIN + CTX), F32),
            pltpu.VMEM((2 * NA_GROUP, 2 * GRID_W, NA_WIN + CTX), F32),
            pltpu.VMEM((2 * NA_GROUP, 2 * GRID_W, 128), F32),
            pltpu.VMEM((2 * NA_KH - 1, 2 * GRID_W, 128), F32),
            pltpu.VMEM((NA_KH, 2 * GRID_W, NA_WIN), F32),
        ],
        compiler_params=_params("parallel", "arbitrary"),
        name="na_attention",
    )(*args)
    return (out[0], out[1]) if need_ctx else (out[0], None)


def na_rpb_rows(rpb):
    lpad = GRID_W - NA_KW
    p = jnp.pad(rpb * LOG2E, ((0, 0), (0, 1), (lpad, 128 - lpad - (2 * NA_KW - 1))))
    return p.reshape(NA_PAIRS, 2, 2 * NA_KH, 128)


def _proj_res_kernel(a_ref, w_ref, b_ref, gate_ref, res_ref, o_ref):
    y = _dot(a_ref[...], w_ref[...]) + b_ref[...]
    o_ref[...] = res_ref[...] + gate_ref[0] * y


def proj_residual(a, w, b, gate, res, *, layer, tm, row_of):
    m, k = a.shape
    return pl.pallas_call(
        _proj_res_kernel,
        grid=(m // tm,),
        in_specs=[
            pl.BlockSpec((tm, k), lambda i: (i, 0)),
            pl.BlockSpec((None, k, D), lambda i: (layer, 0, 0)),
            _vec_spec(D), _mod_spec(row_of),
            pl.BlockSpec((tm, D), lambda i: (i, 0)),
        ],
        out_specs=pl.BlockSpec((tm, D), lambda i: (i, 0)),
        out_shape=jax.ShapeDtypeStruct((m, D), F32),
        compiler_params=_params("parallel"),
        name="proj_residual",
    )(a, w, b, gate, res)


def _ffn_kernel(xp_ref, x_ref, xn_ref, sh_ref, sc_ref, gate_ref, gain_ref, wup_ref, cw_ref, cb_ref, wd_ref,
                o_ref, h_scr, *, tm, tiles_per_seq):
    hl = 16
    gain, shift, scale = gain_ref[...], sh_ref[0], sc_ref[0]
    t = pl.program_id(0) % tiles_per_seq
    hp = _mod_norm(xp_ref[...], gain, shift, scale)
    hn = _mod_norm(xn_ref[...], gain, shift, scale)
    h_scr[0:hl] = jnp.where(t == 0, 0.0, hp).astype(BF16)
    h_scr[hl:hl + tm] = _mod_norm(x_ref[...], gain, shift, scale).astype(BF16)
    h_scr[hl + tm:] = jnp.where(t == tiles_per_seq - 1, 0.0, hn).astype(BF16)

    acc = None
    for c0, c1 in zip(FFN_SPLITS[:-1], FFN_SPLITS[1:]):
        u = _dot(h_scr[...], wup_ref[:, c0:c1])
        v = _dot(h_scr[hl:hl + tm], wup_ref[:, FFN_H + c0:FFN_H + c1])
        uc = (cw_ref[0:1, c0:c1] * u[hl - 1:hl - 1 + tm] + cw_ref[1:2, c0:c1] * u[hl:hl + tm]
              + cw_ref[2:3, c0:c1] * u[hl + 1:hl + 1 + tm] + cb_ref[:, c0:c1])
        g = (jax.nn.gelu(uc, approximate=True) * v).astype(BF16)
        part = _dot(g, wd_ref[c0:c1, :])
        acc = part if acc is None else acc + part
    o_ref[...] = x_ref[...] + gate_ref[0] * acc


def conv_ffn(x, sh, sc, gate, gain, w_up, cw, cb, wd, *, layer, tm, seq, row_of):
    m = x.shape[0]
    hb = tm // 16
    last_hb = m // 16 - 1

    def resident(shape):
        return pl.BlockSpec((None,) + shape, lambda i: (layer, 0, 0), pipeline_mode=pl.Buffered(1))

    return pl.pallas_call(
        functools.partial(_ffn_kernel, tm=tm, tiles_per_seq=seq // tm),
        grid=(m // tm,),
        in_specs=[
            pl.BlockSpec((16, D), lambda i: (jnp.maximum(i * hb - 1, 0), 0)),
            pl.BlockSpec((tm, D), lambda i: (i, 0)),
            pl.BlockSpec((16, D), lambda i: (jnp.minimum((i + 1) * hb, last_hb), 0)),
            _mod_spec(row_of), _mod_spec(row_of), _mod_spec(row_of), _vec_spec(D),
            resident((D, 2 * FFN_H)), resident((3, FFN_H)), resident((1, FFN_H)), resident((FFN_H, D)),
        ],
        out_specs=pl.BlockSpec((tm, D), lambda i: (i, 0)),
        out_shape=jax.ShapeDtypeStruct((m, D), F32),
        scratch_shapes=[pltpu.VMEM((tm + 32, D), BF16)],
        compiler_params=_params("parallel"),
        name="conv_ffn",
    )(x, x, x, sh, sc, gate, gain, w_up, cw, cb, wd)


def _ret_proj_kernel(x_ref, sh_ref, sc_ref, gain_ref, w_ref, cos_ref, sin_ref, o_ref, *, rope):
    h = _mod_norm(x_ref[...], gain_ref[...], sh_ref[0], sc_ref[0]).astype(BF16)
    qk_width = 2 * RET_HEADS * RET_DK
    v_end = qk_width + RET_HEADS * RET_DV
    k_scale = RET_DK ** -0.5
    for hh in range(2 * RET_HEADS):
        y = _dot(h, w_ref[:, hh * RET_DK:(hh + 1) * RET_DK])
        s = k_scale if hh >= RET_HEADS else 1.0
        x1, x2 = y[:, :128], y[:, 128:]
        if rope:
            cos, sin = cos_ref[...], sin_ref[...]
            x1, x2 = x1 * cos - x2 * sin, x1 * sin + x2 * cos
        o_ref[:, hh * RET_DK: hh * RET_DK + 128] = (x1 * s).astype(o_ref.dtype)
        o_ref[:, hh * RET_DK + 128: (hh + 1) * RET_DK] = (x2 * s).astype(o_ref.dtype)
    o_ref[:, qk_width:v_end] = _dot(h, w_ref[:, qk_width:v_end]).astype(o_ref.dtype)
    g = _dot(h, w_ref[:, v_end:])
    o_ref[:, v_end:] = (g * jax.nn.sigmoid(g)).astype(o_ref.dtype)


def ret_proj(x, sh, sc, gain, w, cos, sin, *, tm, seq, row_of, rope):
    m = x.shape[0]
    n = w.shape[1]
    tps = seq // tm
    return pl.pallas_call(
        functools.partial(_ret_proj_kernel, rope=rope),
        grid=(m // tm,),
        in_specs=[
            pl.BlockSpec((tm, D), lambda i: (i, 0)),
            _mod_spec(row_of), _mod_spec(row_of), _vec_spec(D),
            pl.BlockSpec((D, n), lambda i: (0, 0), pipeline_mode=pl.Buffered(1)),
            pl.BlockSpec((tm, 128), lambda i: (i % tps, 0)),
            pl.BlockSpec((tm, 128), lambda i: (i % tps, 0)),
        ],
        out_specs=pl.BlockSpec((tm, n), lambda i: (i, 0)),
        out_shape=jax.ShapeDtypeStruct((m, n), BF16),
        compiler_params=_params("parallel"),
        name="ret_proj",
    )(x, sh, sc, gain, w, cos, sin)


def _log_sigmoid(x):
    return jnp.minimum(x, 0.0) - jnp.log1p(jnp.exp(-jnp.abs(x)))


def _ret_scan_kernel(qx_ref, kx_ref, vx_ref, qc_ref, kc_ref, vc_ref, dl_ref, ox_ref, oc_ref,
                     sf_scr, sb_scr, ax_scr, mask_scr):
    C = RET_C
    n = SEQ // C
    pos = lax.broadcasted_iota(jnp.int32, (C, 128), 0).astype(F32)
    ii = lax.broadcasted_iota(jnp.int32, (C, C), 0).astype(F32)
    jj = lax.broadcasted_iota(jnp.int32, (C, C), 1).astype(F32)

    def lanes(x, width):
        return jnp.concatenate([x] * (width // 128), axis=1)

    def decays(direction):
        lg = jnp.broadcast_to(_log_sigmoid(dl_ref[direction, 0])[0:1, :], (C, 128))
        diff = ii - jj if direction == 0 else jj - ii
        p = pos if direction == 0 else (C - 1.0) - pos
        intra = jnp.where(diff >= 0, jnp.exp(jnp.maximum(diff, 0.0) * lanes(lg, C)), 0.0)
        qd = jnp.exp((p + 1.0) * lg)
        kd = jnp.exp((C - 1.0 - p) * lg)
        cd = jnp.exp(C * lg)[0:1]
        return intra, qd, kd, cd

    intra_f, qd_f, kd_f, cd_f = decays(0)
    intra_b, qd_b, kd_b, cd_b = decays(1)
    mask_scr[...] = intra_f + intra_b

    def group_norm(o):
        mu = jnp.mean(o, axis=-1, keepdims=True)
        oc = o - mu
        return oc * lax.rsqrt(jnp.mean(oc * oc, axis=-1, keepdims=True) + EPS)

    def intra_term(q, k, v):
        return _dot((_dot_nt(q, k) * mask_scr[...]).astype(BF16), v)

    def cross_term(q, s_scr, qd):
        return lanes(qd, RET_DV) * _dot(q, s_scr[...].astype(BF16))

    def state_term(k, v, kd):
        return _dot_tn((k.astype(F32) * lanes(kd, RET_DK)).astype(BF16), v)

    def advance(s_scr, k, v, kd, cd):
        s_scr[...] = s_scr[...] * lanes(cd, RET_DV) + state_term(k, v, kd)

    q, k, v = qc_ref[...], kc_ref[...], vc_ref[...]
    oc_ref[...] = group_norm(intra_term(q, k, v)).astype(oc_ref.dtype)
    sf_scr[...] = state_term(k, v, kd_f)
    sb_scr[...] = state_term(k, v, kd_b)

    def visit(t, finish):
        rf = pl.multiple_of(t * C, C)
        rb = pl.multiple_of((n - 1 - t) * C, C)
        q, k, v = qx_ref[pl.ds(rf, C), :], kx_ref[pl.ds(rf, C), :], vx_ref[pl.ds(rf, C), :]
        o = intra_term(q, k, v) + cross_term(q, sf_scr, qd_f)
        if finish:
            ox_ref[pl.ds(rf, C), :] = group_norm(ax_scr[pl.ds(rf, C), :] + o).astype(ox_ref.dtype)
        else:
            ax_scr[pl.ds(rf, C), :] = o
        advance(sf_scr, k, v, kd_f, cd_f)
        q, k, v = qx_ref[pl.ds(rb, C), :], kx_ref[pl.ds(rb, C), :], vx_ref[pl.ds(rb, C), :]
        o = cross_term(q, sb_scr, qd_b)
        if finish:
            ox_ref[pl.ds(rb, C), :] = group_norm(ax_scr[pl.ds(rb, C), :] + o).astype(ox_ref.dtype)
        else:
            ax_scr[pl.ds(rb, C), :] = o
        advance(sb_scr, k, v, kd_b, cd_b)

    def first_visits(t, carry):
        visit(t, False)
        return carry

    def second_visits(t, carry):
        visit(t, True)
        return carry

    lax.fori_loop(0, n // 2, first_visits, 0)
    lax.fori_loop(n // 2, n, second_visits, 0)


def ret_scan(px, pc, dl_tiles):
    kb = RET_HEADS
    vb = 2 * RET_HEADS * RET_DK // RET_DV
    return pl.pallas_call(
        _ret_scan_kernel,
        grid=(BATCH, RET_HEADS),
        in_specs=[
            pl.BlockSpec((SEQ, RET_DK), lambda b, h: (b, h)),
            pl.BlockSpec((SEQ, RET_DK), lambda b, h: (b, kb + h)),
            pl.BlockSpec((SEQ, RET_DV), lambda b, h: (b, vb + h)),
            pl.BlockSpec((CTX, RET_DK), lambda b, h: (b, h)),
            pl.BlockSpec((CTX, RET_DK), lambda b, h: (b, kb + h)),
            pl.BlockSpec((CTX, RET_DV), lambda b, h: (b, vb + h)),
            pl.BlockSpec((2, 1, 8, 128), lambda b, h: (0, h, 0, 0)),
        ],
        out_specs=[
            pl.BlockSpec((SEQ, RET_DV), lambda b, h: (b, h)),
            pl.BlockSpec((CTX, RET_DV), lambda b, h: (b, h)),
        ],
        out_shape=[
            jax.ShapeDtypeStruct((BATCH * SEQ, RET_HEADS * RET_DV), BF16),
            jax.ShapeDtypeStruct((BATCH * CTX, RET_HEADS * RET_DV), BF16),
        ],
        scratch_shapes=[
            pltpu.VMEM((RET_DK, RET_DV), F32),
            pltpu.VMEM((RET_DK, RET_DV), F32),
            pltpu.VMEM((SEQ, RET_DV), F32),
            pltpu.VMEM((RET_C, RET_C), F32),
        ],
        compiler_params=_params("parallel", "parallel"),
        name="ret_scan",
    )(px, px, px, pc, pc, pc, dl_tiles)


def _ret_out_kernel(o_ref, g_ref, gn_ref, w_ref, gate_ref, res_ref, out_ref):
    a = (o_ref[...].astype(F32) * gn_ref[...] * g_ref[...].astype(F32)).astype(BF16)
    out_ref[...] = res_ref[...] + gate_ref[0] * _dot(a, w_ref[...])


def ret_out(o, proj, gn, w, gate, res, *, tm, row_of):
    m = o.shape[0]
    dv = RET_HEADS * RET_DV
    return pl.pallas_call(
        _ret_out_kernel,
        grid=(m // tm,),
        in_specs=[
            pl.BlockSpec((tm, dv), lambda i: (i, 0)),
            pl.BlockSpec((tm, dv), lambda i: (i, 2)),
            _vec_spec(dv),
            pl.BlockSpec((dv, D), lambda i: (0, 0)),
            _mod_spec(row_of),
            pl.BlockSpec((tm, D), lambda i: (i, 0)),
        ],
        out_specs=pl.BlockSpec((tm, D), lambda i: (i, 0)),
        out_shape=jax.ShapeDtypeStruct((m, D), F32),
        compiler_params=_params("parallel"),
        name="ret_out",
    )(o, proj, gn, w, gate, res)


def _conv_mixer_kernel(xp_ref, x_ref, xn_ref, sh_ref, sc_ref, gate_ref, gain_ref, w1_ref, b1_ref, cw_ref, cb_ref,
                       lg_ref, lb_ref, w2_ref, b2_ref, o_ref, h_scr, ue_scr, y_scr, *, tm, tiles_per_seq):
    hl = CONV_HALO
    gain, shift, scale = gain_ref[...], sh_ref[0], sc_ref[0]
    t = pl.program_id(0) % tiles_per_seq
    h_scr[0:hl] = _mod_norm(xp_ref[...], gain, shift, scale).astype(BF16)
    h_scr[hl:hl + tm] = _mod_norm(x_ref[...], gain, shift, scale).astype(BF16)
    h_scr[hl + tm:] = _mod_norm(xn_ref[...], gain, shift, scale).astype(BF16)
    row = lax.broadcasted_iota(jnp.int32, (tm + 2 * hl, 1), 0)
    inside = ((row >= hl) | (t > 0)) & ((row < hl + tm) | (t < tiles_per_seq - 1))

    base = hl - CONV_W // 2
    win = CONV_RB + 2 * hl
    cw_blk = 256
    for cblk in range(D // cw_blk):
        c0 = cblk * cw_blk
        a = _dot(h_scr[...], w1_ref[:, c0:c0 + cw_blk]) + b1_ref[:, c0:c0 + cw_blk]
        g = _dot(h_scr[...], w1_ref[:, D + c0:D + c0 + cw_blk]) + b1_ref[:, D + c0:D + c0 + cw_blk]
        ue_scr[:, c0:c0 + cw_blk] = jnp.where(inside, a * jax.nn.sigmoid(g), 0.0)
        for sub in range(cw_blk // 128):
            cols = slice(c0 + sub * 128, c0 + (sub + 1) * 128)
            for rb in range(tm // CONV_RB):
                r0 = rb * CONV_RB
                window = ue_scr[r0:r0 + win, cols]
                acc = jnp.zeros((CONV_RB, 128), F32)
                for b in range(8):
                    shifted = window if b == 0 else pltpu.roll(window, win - b, axis=0)
                    for a8 in range((base + CONV_W - 1) // 8 + 1):
                        k = 8 * a8 + b - base
                        if 0 <= k < CONV_W:
                            acc = acc + cw_ref[k:k + 1, cols] * shifted[8 * a8:8 * a8 + CONV_RB]
                y_scr[r0:r0 + CONV_RB, cols] = acc + cb_ref[:, cols]

    y = y_scr[...]
    mu = jnp.mean(y, axis=-1, keepdims=True)
    yc = y - mu
    var = jnp.mean(yc * yc, axis=-1, keepdims=True)
    z = yc * lax.rsqrt(var + EPS) * lg_ref[...] + lb_ref[...]
    z = (z * jax.nn.sigmoid(z)).astype(BF16)
    o_ref[...] = x_ref[...] + gate_ref[0] * (_dot(z, w2_ref[...]) + b2_ref[...])


def conv_mixer(x, sh, sc, gate, gain, w1, b1, cw, cb, lg, lb, w2, b2, *, tm, seq, row_of):
    m = x.shape[0]
    hb = tm // CONV_HALO
    last_hb = m // CONV_HALO - 1

    def resident(shape):
        return pl.BlockSpec(shape, lambda i: (0, 0), pipeline_mode=pl.Buffered(1))

    return pl.pallas_call(
        functools.partial(_conv_mixer_kernel, tm=tm, tiles_per_seq=seq // tm),
        grid=(m // tm,),
        in_specs=[
            pl.BlockSpec((CONV_HALO, D), lambda i: (jnp.maximum(i * hb - 1, 0), 0)),
            pl.BlockSpec((tm, D), lambda i: (i, 0)),
            pl.BlockSpec((CONV_HALO, D), lambda i: (jnp.minimum((i + 1) * hb, last_hb), 0)),
            _mod_spec(row_of), _mod_spec(row_of), _mod_spec(row_of), _vec_spec(D),
            resident((D, 2 * D)), _vec_spec(2 * D),
            resident((CONV_W, D)), _vec_spec(D), _vec_spec(D), _vec_spec(D),
            resident((D, D)), _vec_spec(D),
        ],
        out_specs=pl.BlockSpec((tm, D), lambda i: (i, 0)),
        out_shape=jax.ShapeDtypeStruct((m, D), F32),
        scratch_shapes=[
            pltpu.VMEM((tm + 2 * CONV_HALO, D), BF16),
            pltpu.VMEM((tm + 2 * CONV_HALO, D), F32),
            pltpu.VMEM((tm, D), F32),
        ],
        compiler_params=_params("parallel"),
        name="conv_mixer",
    )(x, x, x, sh, sc, gate, gain, w1, b1, cw, cb, lg, lb, w2, b2)


def _rope_tables():
    t = np.arange(SEQ)
    row = (t // GRID_W).astype(np.float32)
    col = (t % GRID_W).astype(np.float32)
    n_freq = RET_DK // 4
    inv_freq = jnp.power(ROPE_BASE, -jnp.arange(n_freq, dtype=F32) / n_freq)
    ang = jnp.concatenate([row[:, None] * inv_freq, col[:, None] * inv_freq], axis=-1)
    return jnp.cos(ang), jnp.sin(ang)


def kernel(x, c, ctx, c_ctx, ada_w, ada_b, norm_mix, norm_ffn, na_w_qkv, na_w_o, na_q_gain, na_k_gain, na_rpb, ret_w_q, ret_w_k, ret_w_v, ret_w_g, ret_w_o, ret_gn_gain, ret_decay_logit, cv_w_pw1, cv_b_pw1, cv_w_dw, cv_b_dw, cv_ln_g, cv_ln_b, cv_w_pw2, cv_b_pw2, ffn_w_up, ffn_w_dw, ffn_b_dw, ffn_w_down):
    xs = x.reshape(BATCH * SEQ, D)
    cs = ctx.reshape(BATCH * CTX, D)
    tm_x, tm_c = 512, CTX
    tiles_per_batch = SEQ // tm_x

    def x_row(i):
        return i // tiles_per_batch

    def c_row(i):
        return CTX_ROW

    c_rows = jnp.zeros((MOD_ROWS, D), F32).at[:BATCH].set(c).at[CTX_ROW].set(c_ctx)
    mods = ada_table(c_rows, ada_w, ada_b).reshape(DEPTH, MOD_ROWS, 6, 1, D)
    zero_bias = jnp.zeros((1, D), F32)
    cos, sin = _rope_tables()
    w_qkv, w_o_na = na_w_qkv.astype(BF16), na_w_o.astype(BF16)
    w_up, w_down = ffn_w_up.astype(BF16), ffn_w_down.astype(BF16)
    ffn_cb = ffn_b_dw.reshape(DEPTH, 1, FFN_H)

    for i in range(DEPTH):
        need_ctx = i < DEPTH - 1
        mod = [mods[i, :, k] for k in range(6)]
        g_mix = norm_mix[i].reshape(1, D)
        g_ffn = norm_ffn[i].reshape(1, D)
        kind, j = i % 3, i // 3
        if kind == 0:
            qg = (jnp.tile(na_q_gain[j], NA_HEADS) * (NA_SCALE * LOG2E)).reshape(1, D)
            kg = jnp.tile(na_k_gain[j], NA_HEADS).reshape(1, D)
            qkv_x = na_qkv(xs, mod[0], mod[1], g_mix, w_qkv, qg, kg, layer=j, tm=tm_x, row_of=x_row)
            qkv_c = na_qkv(cs, mod[0], mod[1], g_mix, w_qkv, qg, kg, layer=j, tm=tm_c, row_of=c_row)
            ox, oc = na_attention(qkv_x, qkv_c, na_rpb_rows(na_rpb[j]), need_ctx=need_ctx)
            xs = proj_residual(ox, w_o_na, zero_bias, mod[2], xs, layer=j, tm=tm_x, row_of=x_row)
            if need_ctx:
                cs = proj_residual(oc, w_o_na, zero_bias, mod[2], cs, layer=j, tm=tm_c, row_of=c_row)
        elif kind == 1:
            w = jnp.concatenate([ret_w_q[j], ret_w_k[j], ret_w_v[j], ret_w_g[j]], axis=1).astype(BF16)
            px = ret_proj(xs, mod[0], mod[1], g_mix, w, cos, sin, tm=tm_x, seq=SEQ, row_of=x_row, rope=True)
            pc = ret_proj(cs, mod[0], mod[1], g_mix, w, cos[:CTX], sin[:CTX], tm=tm_c, seq=CTX, row_of=c_row,
                          rope=False)
            dl = jnp.broadcast_to(ret_decay_logit[j][:, :, None, None], (2, RET_HEADS, 8, 128))
            ox, oc = ret_scan(px, pc, dl)
            w_o = ret_w_o[j].astype(BF16)
            gn = ret_gn_gain[j].reshape(1, RET_HEADS * RET_DV)
            xs = ret_out(ox, px, gn, w_o, mod[2], xs, tm=tm_x, row_of=x_row)
            if need_ctx:
                cs = ret_out(oc, pc, gn, w_o, mod[2], cs, tm=tm_c, row_of=c_row)
        else:
            conv_args = (mod[0], mod[1], mod[2], g_mix, cv_w_pw1[j].astype(BF16), cv_b_pw1[j].reshape(1, 2 * D),
                         cv_w_dw[j], cv_b_dw[j].reshape(1, D), cv_ln_g[j].reshape(1, D), cv_ln_b[j].reshape(1, D),
                         cv_w_pw2[j].astype(BF16), cv_b_pw2[j].reshape(1, D))
            xs = conv_mixer(xs, *conv_args, tm=tm_x, seq=SEQ, row_of=x_row)
            if need_ctx:
                cs = conv_mixer(cs, *conv_args, tm=tm_c, seq=CTX, row_of=c_row)

        ffn_args = (mod[3], mod[4], mod[5], g_ffn, w_up, ffn_w_dw, ffn_cb, w_down)
        xs = conv_ffn(xs, *ffn_args, layer=i, tm=tm_x, seq=SEQ, row_of=x_row)
        if need_ctx:
            cs = conv_ffn(cs, *ffn_args, layer=i, tm=tm_c, seq=CTX, row_of=c_row)

    return xs.reshape(BATCH, SEQ, D)
```

```python
import functools

import numpy as np
import jax
import jax.numpy as jnp
from jax import lax
from jax.experimental import pallas as pl
from jax.experimental.pallas import tpu as pltpu

D = 1024
BATCH = 4
SEQ = 4096
DEPTH = 4
CTX = 256
GRID_W = 64
ROWS = SEQ // GRID_W
EPS = 1e-6

NA_HEADS = 16
NA_DH = 64
NA_KH = 8
NA_KW = 16
NA_SCALE = NA_DH ** -0.5
NA_PAIRS = NA_HEADS // 2
NA_WIN = NA_KH * GRID_W
NA_GROUP = 4
NA_CB = NA_KW
NA_NCB = GRID_W // NA_CB

RET_HEADS = 4
RET_DK = 256
RET_DV = 512
RET_C = 256
assert CTX == RET_C and (SEQ // RET_C) % 2 == 0
ROPE_BASE = 10000.0

CONV_W = 31
CONV_HALO = 16
CONV_RB = 128
FFN_H = 2816
FFN_SPLITS = (0, FFN_H)

MOD_ROWS = 8
CTX_ROW = BATCH
NEG = -1e30
LOG2E = 1.4426950408889634

F32 = jnp.float32
BF16 = jnp.bfloat16

VMEM_LIMIT = 52 * 1024 * 1024


def _params(*sem):
    return pltpu.CompilerParams(dimension_semantics=sem, vmem_limit_bytes=VMEM_LIMIT)


def _mod_norm(x, gain, shift, scale):
    ms = jnp.mean(x * x, axis=-1, keepdims=True)
    return x * lax.rsqrt(ms + EPS) * gain * (1.0 + scale) + shift


def _dot(a, b):
    return jnp.dot(a, b, preferred_element_type=F32)


def _dot_nt(a, b):
    return lax.dot_general(a, b, (((1,), (1,)), ((), ())), preferred_element_type=F32)


def _dot_tn(a, b):
    return lax.dot_general(a, b, (((0,), (0,)), ((), ())), preferred_element_type=F32)


def _ada_kernel(c_ref, w_ref, b_ref, o_ref):
    c = c_ref[...]
    s = (c * jax.nn.sigmoid(c)).astype(BF16)
    o_ref[0] = _dot(s, w_ref[0].astype(BF16)) + b_ref[0]


def ada_table(c_rows, ada_w, ada_b):
    tn = 1536
    return pl.pallas_call(
        _ada_kernel,
        grid=(DEPTH, 6 * D // tn),
        in_specs=[
            pl.BlockSpec((MOD_ROWS, D), lambda l, j: (0, 0)),
            pl.BlockSpec((1, D, tn), lambda l, j: (l, 0, j)),
            pl.BlockSpec((1, 1, tn), lambda l, j: (l, 0, j)),
        ],
        out_specs=pl.BlockSpec((1, MOD_ROWS, tn), lambda l, j: (l, 0, j)),
        out_shape=jax.ShapeDtypeStruct((DEPTH, MOD_ROWS, 6 * D), F32),
        compiler_params=_params("parallel", "parallel"),
        name="ada_table",
    )(c_rows, ada_w, ada_b.reshape(DEPTH, 1, 6 * D))


def _mod_spec(row_of):
    return pl.BlockSpec((1, 1, D), lambda i, *_: (row_of(i), 0, 0))


def _vec_spec(n):
    return pl.BlockSpec((1, n), lambda i, *_: (0, 0))


def _na_qkv_kernel(x_ref, sh_ref, sc_ref, gain_ref, w_ref, qg_ref, kg_ref, o_ref):
    h = _mod_norm(x_ref[...], gain_ref[...], sh_ref[0], sc_ref[0]).astype(BF16)
    gw = 256
    r = lax.broadcasted_iota(jnp.int32, (gw, gw), 0) // NA_DH
    c = lax.broadcasted_iota(jnp.int32, (gw, gw), 1) // NA_DH
    seg = jnp.where(r == c, 1.0, 0.0).astype(BF16)
    for sec, g_ref in ((0, qg_ref), (1, kg_ref)):
        y = _dot(h, w_ref[:, sec * D:(sec + 1) * D])
        for g in range(D // gw):
            yg = y[:, g * gw:(g + 1) * gw]
            ss = _dot((yg * yg).astype(BF16), seg)
            yn = yg * lax.rsqrt(ss * (1.0 / NA_DH) + EPS) * g_ref[:, g * gw:(g + 1) * gw]
            o_ref[:, sec * D + g * gw: sec * D + (g + 1) * gw] = yn.astype(o_ref.dtype)
    o_ref[:, 2 * D:] = _dot(h, w_ref[:, 2 * D:]).astype(o_ref.dtype)


def na_qkv(x, sh, sc, gain, w, qg, kg, *, layer, tm, row_of):
    m = x.shape[0]
    return pl.pallas_call(
        _na_qkv_kernel,
        grid=(m // tm,),
        in_specs=[
            pl.BlockSpec((tm, D), lambda i: (i, 0)),
            _mod_spec(row_of), _mod_spec(row_of), _vec_spec(D),
            pl.BlockSpec((None, D, 3 * D), lambda i: (layer, 0, 0)),
            _vec_spec(D), _vec_spec(D),
        ],
        out_specs=pl.BlockSpec((tm, 3 * D), lambda i: (i, 0)),
        out_shape=jax.ShapeDtypeStruct((m, 3 * D), BF16),
        compiler_params=_params("parallel"),
        name="na_qkv",
    )(x, sh, sc, gain, w, qg, kg)


def _split_heads(q, in_a):
    zero = jnp.zeros_like(q)
    return jnp.concatenate([jnp.where(in_a, q, zero), jnp.where(in_a, zero, q)], axis=0)


def _live_key_blocks(group):
    lo = min(max(8 * group - NA_KW // 2, 0), GRID_W - NA_KW)
    hi = min(max(8 * group + 7 - NA_KW // 2, 0), GRID_W - NA_KW) + NA_KW - 1
    return lo // NA_CB, hi // NA_CB + 1


def _na_attn_kernel(q_ref, k_ref, v_ref, kc_ref, vc_ref, rpb_ref, *rest, need_ctx):
    if need_ctx:
        qc_ref, o_ref, oc_ref, kb_scr, vb_scr, s_scr, p_scr, l_scr, toep_scr, bias_scr = rest
    else:
        o_ref, kb_scr, vb_scr, s_scr, p_scr, l_scr, toep_scr, bias_scr = rest
    in_a = lax.broadcasted_iota(jnp.int32, (1, 128), 1) < NA_DH
    kc = kc_ref[...]
    vc = vc_ref[...]
    blk = NA_KH * NA_CB

    def relayout(row, carry):
        dst = pl.ds(pl.multiple_of(row * NA_CB, NA_CB), NA_CB)
        for cb in range(NA_NCB):
            src = pl.ds(pl.multiple_of(row * GRID_W + cb * NA_CB, NA_CB), NA_CB)
            kb_scr[cb, dst, :] = k_ref[src, :]
            vb_scr[cb, dst, :] = v_ref[src, :]
        return carry

    lax.fori_loop(0, ROWS, relayout, 0)
    p_scr[...] = jnp.zeros_like(p_scr)

    def window(r):
        if isinstance(r, int):
            rs = min(max(r - NA_KH // 2, 0), ROWS - NA_KH)
            return r * GRID_W, rs * NA_CB, r - rs
        rs = jnp.clip(r - NA_KH // 2, 0, ROWS - NA_KH)
        return pl.multiple_of(r * GRID_W, GRID_W), pl.multiple_of(rs * NA_CB, NA_CB), r - rs

    def key_window(ref, w0):
        return jnp.concatenate([ref[cb, pl.ds(w0, blk), :] for cb in range(NA_NCB)], axis=0)

    def live(group):
        b0, b1 = _live_key_blocks(group % (GRID_W // 8))
        return slice(8 * group, 8 * group + 8), slice(b0 * blk, b1 * blk)

    n_groups8 = 2 * GRID_W // 8
    n_rel_r = 2 * NA_KH - 1

    @pl.when(pl.program_id(1) == 0)
    def _():
        c_idx = lax.broadcasted_iota(jnp.int32, (GRID_W, 128), 0)
        kc_idx = lax.broadcasted_iota(jnp.int32, (GRID_W, 128), 1)
        cs = jnp.clip(c_idx - NA_KW // 2, 0, GRID_W - NA_KW)
        valid = (kc_idx >= cs) & (kc_idx < cs + NA_KW)
        for head in range(2):
            for rr in range(n_rel_r):
                row = jnp.broadcast_to(rpb_ref[0, head, rr:rr + 1, :], (GRID_W, 128))
                toep = pltpu.roll(row, 128 - (GRID_W - 1), 1, stride=1, stride_axis=0)
                toep_scr[rr, head * GRID_W:(head + 1) * GRID_W, :] = jnp.where(valid, toep, NEG)

        def per_offset(var, carry):
            for g in range(n_groups8):
                rows, _ = live(g)
                b0, b1 = _live_key_blocks(g % (GRID_W // 8))
                for i in range(NA_KH):
                    src = toep_scr[i - var + NA_KH - 1, rows, :]
                    for cb in range(b0, b1):
                        dst = cb * blk + i * NA_CB
                        bias_scr[var, rows, dst:dst + NA_CB] = src[:, cb * NA_CB:(cb + 1) * NA_CB]
            return carry

        lax.fori_loop(0, NA_KH, per_offset, 0)

    def scores(r, slot):
        q0, w0, var = window(r)
        q2 = _split_heads(q_ref[pl.ds(q0, GRID_W), :], in_a)
        s_loc = _dot_nt(q2, key_window(kb_scr, w0))
        for g in range(n_groups8):
            rows, lanes = live(g)
            s_scr[slot, rows, lanes] = s_loc[rows, lanes] + bias_scr[var, rows, lanes]
        s_scr[slot, :, NA_WIN:] = _dot_nt(q2, kc)

    def softmax(slot):
        for g in range(n_groups8):
            rows, lanes = live(g)
            s = jnp.concatenate([s_scr[slot, rows, lanes], s_scr[slot, rows, NA_WIN:]], axis=1)
            p = jnp.exp2(s - jnp.max(s, axis=-1, keepdims=True))
            l_scr[slot, rows, :] = jnp.broadcast_to(jnp.sum(p, axis=-1, keepdims=True), (8, 128))
            n_loc = lanes.stop - lanes.start
            p_scr[slot, rows, lanes] = p[:, :n_loc]
            p_scr[slot, rows, NA_WIN:] = p[:, n_loc:]

    def values(r, slot):
        q0, w0, _ = window(r)
        p = p_scr[slot].astype(BF16)
        v = jnp.concatenate([key_window(vb_scr, w0), vc], axis=0)
        o2 = _dot(p, v) / l_scr[slot]
        o_ref[pl.ds(q0, GRID_W), :] = jnp.where(in_a, o2[:GRID_W], o2[GRID_W:]).astype(o_ref.dtype)

    def stage_values(t):
        for u in range(NA_GROUP):
            values(NA_GROUP * t + u, NA_GROUP * (t % 2) + u)

    def stage_softmax(t):
        for u in range(NA_GROUP):
            softmax(NA_GROUP * (t % 2) + u)

    def stage_scores(t):
        for u in range(NA_GROUP):
            scores(NA_GROUP * t + u, NA_GROUP * (t % 2) + u)

    n_groups = ROWS // NA_GROUP
    stage_scores(0)
    stage_softmax(0)
    stage_scores(1)

    def step(t, carry):
        stage_values(t - 2)
        stage_softmax(t - 1)
        stage_scores(t)
        return carry

    lax.fori_loop(2, n_groups, step, 0)
    stage_values(n_groups - 2)
    stage_softmax(n_groups - 1)
    stage_values(n_groups - 1)

    if need_ctx:
        q2 = _split_heads(qc_ref[...], in_a)
        s = _dot_nt(q2, kc)
        p = jnp.exp2(s - jnp.max(s, axis=-1, keepdims=True))
        o2 = _dot(p.astype(BF16), vc) / jnp.sum(p, axis=-1, keepdims=True)
        oc_ref[...] = jnp.where(in_a, o2[:CTX], o2[CTX:]).astype(oc_ref.dtype)


def na_attention(qkv_x, qkv_c, rpb, *, need_ctx):
    def col(sec):
        return lambda p, b: (b, sec * NA_PAIRS + p)
    in_specs = [
        pl.BlockSpec((SEQ, 128), col(0)),
        pl.BlockSpec((SEQ, 128), col(1)),
        pl.BlockSpec((SEQ, 128), col(2)),
        pl.BlockSpec((CTX, 128), col(1)),
        pl.BlockSpec((CTX, 128), col(2)),
        pl.BlockSpec((1, 2, 2 * NA_KH, 128), lambda p, b: (p, 0, 0, 0)),
    ]
    args = [qkv_x, qkv_x, qkv_x, qkv_c, qkv_c, rpb]
    out_specs = [pl.BlockSpec((SEQ, 128), lambda p, b: (b, p))]
    out_shape = [jax.ShapeDtypeStruct((BATCH * SEQ, D), BF16)]
    if need_ctx:
        in_specs.append(pl.BlockSpec((CTX, 128), col(0)))
        args.append(qkv_c)
        out_specs.append(pl.BlockSpec((CTX, 128), lambda p, b: (b, p)))
        out_shape.append(jax.ShapeDtypeStruct((BATCH * CTX, D), BF16))
    out = pl.pallas_call(
        functools.partial(_na_attn_kernel, need_ctx=need_ctx),
        grid=(NA_PAIRS, BATCH),
        in_specs=in_specs,
        out_specs=out_specs,
        out_shape=out_shape,
        scratch_shapes=[
            pltpu.VMEM((NA_NCB, ROWS * NA_CB, 128), BF16),
            pltpu.VMEM((NA_NCB, ROWS * NA_CB, 128), BF16),
            pltpu.VMEM((2 * NA_GROUP, 2 * GRID_W, NA_WIN + CTX), F32),
            pltpu.VMEM((2 * NA_GROUP, 2 * GRID_W, NA_WIN + CTX), F32),
            pltpu.VMEM((2 * NA_GROUP, 2 * GRID_W, 128), F32),
            pltpu.VMEM((2 * NA_KH - 1, 2 * GRID_W, 128), F32),
            pltpu.VMEM((NA_KH, 2 * GRID_W, NA_WIN), F32),
        ],
        compiler_params=_params("parallel", "arbitrary"),
        name="na_attention",
    )(*args)
    return (out[0], out[1]) if need_ctx else (out[0], None)


def na_rpb_rows(rpb):
    lpad = GRID_W - NA_KW
    p = jnp.pad(rpb * LOG2E, ((0, 0), (0, 1), (lpad, 128 - lpad - (2 * NA_KW - 1))))
    return p.reshape(NA_PAIRS, 2, 2 * NA_KH, 128)


def _proj_res_kernel(a_ref, w_ref, b_ref, gate_ref, res_ref, o_ref):
    y = _dot(a_ref[...], w_ref[...]) + b_ref[...]
    o_ref[...] = res_ref[...] + gate_ref[0] * y


def proj_residual(a, w, b, gate, res, *, layer, tm, row_of):
    m, k = a.shape
    return pl.pallas_call(
        _proj_res_kernel,
        grid=(m // tm,),
        in_specs=[
            pl.BlockSpec((tm, k), lambda i: (i, 0)),
            pl.BlockSpec((None, k, D), lambda i: (layer, 0, 0)),
            _vec_spec(D), _mod_spec(row_of),
            pl.BlockSpec((tm, D), lambda i: (i, 0)),
        ],
        out_specs=pl.BlockSpec((tm, D), lambda i: (i, 0)),
        out_shape=jax.ShapeDtypeStruct((m, D), F32),
        compiler_params=_params("parallel"),
        name="proj_residual",
    )(a, w, b, gate, res)


def _ffn_kernel(xp_ref, x_ref, xn_ref, sh_ref, sc_ref, gate_ref, gain_ref, wup_ref, cw_ref, cb_ref, wd_ref,
                o_ref, h_scr, *, tm, tiles_per_seq):
    hl = 16
    gain, shift, scale = gain_ref[...], sh_ref[0], sc_ref[0]
    t = pl.program_id(0) % tiles_per_seq
    hp = _mod_norm(xp_ref[...], gain, shift, scale)
    hn = _mod_norm(xn_ref[...], gain, shift, scale)
    h_scr[0:hl] = jnp.where(t == 0, 0.0, hp).astype(BF16)
    h_scr[hl:hl + tm] = _mod_norm(x_ref[...], gain, shift, scale).astype(BF16)
    h_scr[hl + tm:] = jnp.where(t == tiles_per_seq - 1, 0.0, hn).astype(BF16)

    acc = None
    for c0, c1 in zip(FFN_SPLITS[:-1], FFN_SPLITS[1:]):
        u = _dot(h_scr[...], wup_ref[:, c0:c1])
        v = _dot(h_scr[hl:hl + tm], wup_ref[:, FFN_H + c0:FFN_H + c1])
        uc = (cw_ref[0:1, c0:c1] * u[hl - 1:hl - 1 + tm] + cw_ref[1:2, c0:c1] * u[hl:hl + tm]
              + cw_ref[2:3, c0:c1] * u[hl + 1:hl + 1 + tm] + cb_ref[:, c0:c1])
        g = (jax.nn.gelu(uc, approximate=True) * v).astype(BF16)
        part = _dot(g, wd_ref[c0:c1, :])
        acc = part if acc is None else acc + part
    o_ref[...] = x_ref[...] + gate_ref[0] * acc


def conv_ffn(x, sh, sc, gate, gain, w_up, cw, cb, wd, *, layer, tm, seq, row_of):
    m = x.shape[0]
    hb = tm // 16
    last_hb = m // 16 - 1

    def resident(shape):
        return pl.BlockSpec((None,) + shape, lambda i: (layer, 0, 0), pipeline_mode=pl.Buffered(1))

    return pl.pallas_call(
        functools.partial(_ffn_kernel, tm=tm, tiles_per_seq=seq // tm),
        grid=(m // tm,),
        in_specs=[
            pl.BlockSpec((16, D), lambda i: (jnp.maximum(i * hb - 1, 0), 0)),
            pl.BlockSpec((tm, D), lambda i: (i, 0)),
            pl.BlockSpec((16, D), lambda i: (jnp.minimum((i + 1) * hb, last_hb), 0)),
            _mod_spec(row_of), _mod_spec(row_of), _mod_spec(row_of), _vec_spec(D),
            resident((D, 2 * FFN_H)), resident((3, FFN_H)), resident((1, FFN_H)), resident((FFN_H, D)),
        ],
        out_specs=pl.BlockSpec((tm, D), lambda i: (i, 0)),
        out_shape=jax.ShapeDtypeStruct((m, D), F32),
        scratch_shapes=[pltpu.VMEM((tm + 32, D), BF16)],
        compiler_params=_params("parallel"),
        name="conv_ffn",
    )(x, x, x, sh, sc, gate, gain, w_up, cw, cb, wd)


def _ret_proj_kernel(x_ref, sh_ref, sc_ref, gain_ref, w_ref, cos_ref, sin_ref, o_ref, *, rope):
    h = _mod_norm(x_ref[...], gain_ref[...], sh_ref[0], sc_ref[0]).astype(BF16)
    qk_width = 2 * RET_HEADS * RET_DK
    v_end = qk_width + RET_HEADS * RET_DV
    k_scale = RET_DK ** -0.5
    for hh in range(2 * RET_HEADS):
        y = _dot(h, w_ref[:, hh * RET_DK:(hh + 1) * RET_DK])
        s = k_scale if hh >= RET_HEADS else 1.0
        x1, x2 = y[:, :128], y[:, 128:]
        if rope:
            cos, sin = cos_ref[...], sin_ref[...]
            x1, x2 = x1 * cos - x2 * sin, x1 * sin + x2 * cos
        o_ref[:, hh * RET_DK: hh * RET_DK + 128] = (x1 * s).astype(o_ref.dtype)
        o_ref[:, hh * RET_DK + 128: (hh + 1) * RET_DK] = (x2 * s).astype(o_ref.dtype)
    o_ref[:, qk_width:v_end] = _dot(h, w_ref[:, qk_width:v_end]).astype(o_ref.dtype)
    g = _dot(h, w_ref[:, v_end:])
    o_ref[:, v_end:] = (g * jax.nn.sigmoid(g)).astype(o_ref.dtype)


def ret_proj(x, sh, sc, gain, w, cos, sin, *, tm, seq, row_of, rope):
    m = x.shape[0]
    n = w.shape[1]
    tps = seq // tm
    return pl.pallas_call(
        functools.partial(_ret_proj_kernel, rope=rope),
        grid=(m // tm,),
        in_specs=[
            pl.BlockSpec((tm, D), lambda i: (i, 0)),
            _mod_spec(row_of), _mod_spec(row_of), _vec_spec(D),
            pl.BlockSpec((D, n), lambda i: (0, 0), pipeline_mode=pl.Buffered(1)),
            pl.BlockSpec((tm, 128), lambda i: (i % tps, 0)),
            pl.BlockSpec((tm, 128), lambda i: (i % tps, 0)),
        ],
        out_specs=pl.BlockSpec((tm, n), lambda i: (i, 0)),
        out_shape=jax.ShapeDtypeStruct((m, n), BF16),
        compiler_params=_params("parallel"),
        name="ret_proj",
    )(x, sh, sc, gain, w, cos, sin)


def _log_sigmoid(x):
    return jnp.minimum(x, 0.0) - jnp.log1p(jnp.exp(-jnp.abs(x)))


def _ret_scan_kernel(qx_ref, kx_ref, vx_ref, qc_ref, kc_ref, vc_ref, dl_ref, ox_ref, oc_ref,
                     sf_scr, sb_scr, ax_scr, mask_scr):
    C = RET_C
    n = SEQ // C
    pos = lax.broadcasted_iota(jnp.int32, (C, 128), 0).astype(F32)
    ii = lax.broadcasted_iota(jnp.int32, (C, C), 0).astype(F32)
    jj = lax.broadcasted_iota(jnp.int32, (C, C), 1).astype(F32)

    def lanes(x, width):
        return jnp.concatenate([x] * (width // 128), axis=1)

    def decays(direction):
        lg = jnp.broadcast_to(_log_sigmoid(dl_ref[direction, 0])[0:1, :], (C, 128))
        diff = ii - jj if direction == 0 else jj - ii
        p = pos if direction == 0 else (C - 1.0) - pos
        intra = jnp.where(diff >= 0, jnp.exp(jnp.maximum(diff, 0.0) * lanes(lg, C)), 0.0)
        qd = jnp.exp((p + 1.0) * lg)
        kd = jnp.exp((C - 1.0 - p) * lg)
        cd = jnp.exp(C * lg)[0:1]
        return intra, qd, kd, cd

    intra_f, qd_f, kd_f, cd_f = decays(0)
    intra_b, qd_b, kd_b, cd_b = decays(1)
    mask_scr[...] = intra_f + intra_b

    def group_norm(o):
        mu = jnp.mean(o, axis=-1, keepdims=True)
        oc = o - mu
        return oc * lax.rsqrt(jnp.mean(oc * oc, axis=-1, keepdims=True) + EPS)

    def intra_term(q, k, v):
        return _dot((_dot_nt(q, k) * mask_scr[...]).astype(BF16), v)

    def cross_term(q, s_scr, qd):
        return lanes(qd, RET_DV) * _dot(q, s_scr[...].astype(BF16))

    def state_term(k, v, kd):
        return _dot_tn((k.astype(F32) * lanes(kd, RET_DK)).astype(BF16), v)

    def advance(s_scr, k, v, kd, cd):
        s_scr[...] = s_scr[...] * lanes(cd, RET_DV) + state_term(k, v, kd)

    q, k, v = qc_ref[...], kc_ref[...], vc_ref[...]
    oc_ref[...] = group_norm(intra_term(q, k, v)).astype(oc_ref.dtype)
    sf_scr[...] = state_term(k, v, kd_f)
    sb_scr[...] = state_term(k, v, kd_b)

    def visit(t, finish):
        rf = pl.multiple_of(t * C, C)
        rb = pl.multiple_of((n - 1 - t) * C, C)
        q, k, v = qx_ref[pl.ds(rf, C), :], kx_ref[pl.ds(rf, C), :], vx_ref[pl.ds(rf, C), :]
        o = intra_term(q, k, v) + cross_term(q, sf_scr, qd_f)
        if finish:
            ox_ref[pl.ds(rf, C), :] = group_norm(ax_scr[pl.ds(rf, C), :] + o).astype(ox_ref.dtype)
        else:
            ax_scr[pl.ds(rf, C), :] = o
        advance(sf_scr, k, v, kd_f, cd_f)
        q, k, v = qx_ref[pl.ds(rb, C), :], kx_ref[pl.ds(rb, C), :], vx_ref[pl.ds(rb, C), :]
        o = cross_term(q, sb_scr, qd_b)
        if finish:
            ox_ref[pl.ds(rb, C), :] = group_norm(ax_scr[pl.ds(rb, C), :] + o).astype(ox_ref.dtype)
        else:
            ax_scr[pl.ds(rb, C), :] = o
        advance(sb_scr, k, v, kd_b, cd_b)

    def first_visits(t, carry):
        visit(t, False)
        return carry

    def second_visits(t, carry):
        visit(t, True)
        return carry

    lax.fori_loop(0, n // 2, first_visits, 0, unroll=8)
    lax.fori_loop(n // 2, n, second_visits, 0, unroll=8)


def ret_scan(px, pc, dl_tiles):
    kb = RET_HEADS
    vb = 2 * RET_HEADS * RET_DK // RET_DV
    return pl.pallas_call(
        _ret_scan_kernel,
        grid=(BATCH, RET_HEADS),
        in_specs=[
            pl.BlockSpec((SEQ, RET_DK), lambda b, h: (b, h)),
            pl.BlockSpec((SEQ, RET_DK), lambda b, h: (b, kb + h)),
            pl.BlockSpec((SEQ, RET_DV), lambda b, h: (b, vb + h)),
            pl.BlockSpec((CTX, RET_DK), lambda b, h: (b, h)),
            pl.BlockSpec((CTX, RET_DK), lambda b, h: (b, kb + h)),
            pl.BlockSpec((CTX, RET_DV), lambda b, h: (b, vb + h)),
            pl.BlockSpec((2, 1, 8, 128), lambda b, h: (0, h, 0, 0)),
        ],
        out_specs=[
            pl.BlockSpec((SEQ, RET_DV), lambda b, h: (b, h)),
            pl.BlockSpec((CTX, RET_DV), lambda b, h: (b, h)),
        ],
        out_shape=[
            jax.ShapeDtypeStruct((BATCH * SEQ, RET_HEADS * RET_DV), BF16),
            jax.ShapeDtypeStruct((BATCH * CTX, RET_HEADS * RET_DV), BF16),
        ],
        scratch_shapes=[
            pltpu.VMEM((RET_DK, RET_DV), F32),
            pltpu.VMEM((RET_DK, RET_DV), F32),
            pltpu.VMEM((SEQ, RET_DV), F32),
            pltpu.VMEM((RET_C, RET_C), F32),
        ],
        compiler_params=_params("parallel", "parallel"),
        name="ret_scan",
    )(px, px, px, pc, pc, pc, dl_tiles)


def _ret_out_kernel(o_ref, g_ref, gn_ref, w_ref, gate_ref, res_ref, out_ref):
    a = (o_ref[...].astype(F32) * gn_ref[...] * g_ref[...].astype(F32)).astype(BF16)
    out_ref[...] = res_ref[...] + gate_ref[0] * _dot(a, w_ref[...])


def ret_out(o, proj, gn, w, gate, res, *, tm, row_of):
    m = o.shape[0]
    dv = RET_HEADS * RET_DV
    return pl.pallas_call(
        _ret_out_kernel,
        grid=(m // tm,),
        in_specs=[
            pl.BlockSpec((tm, dv), lambda i: (i, 0)),
            pl.BlockSpec((tm, dv), lambda i: (i, 2)),
            _vec_spec(dv),
            pl.BlockSpec((dv, D), lambda i: (0, 0)),
            _mod_spec(row_of),
            pl.BlockSpec((tm, D), lambda i: (i, 0)),
        ],
        out_specs=pl.BlockSpec((tm, D), lambda i: (i, 0)),
        out_shape=jax.ShapeDtypeStruct((m, D), F32),
        compiler_params=_params("parallel"),
        name="ret_out",
    )(o, proj, gn, w, gate, res)


def _conv_mixer_kernel(xp_ref, x_ref, xn_ref, sh_ref, sc_ref, gate_ref, gain_ref, w1_ref, b1_ref, cw_ref, cb_ref,
                       lg_ref, lb_ref, w2_ref, b2_ref, o_ref, h_scr, ue_scr, y_scr, *, tm, tiles_per_seq):
    hl = CONV_HALO
    gain, shift, scale = gain_ref[...], sh_ref[0], sc_ref[0]
    t = pl.program_id(0) % tiles_per_seq
    h_scr[0:hl] = _mod_norm(xp_ref[...], gain, shift, scale).astype(BF16)
    h_scr[hl:hl + tm] = _mod_norm(x_ref[...], gain, shift, scale).astype(BF16)
    h_scr[hl + tm:] = _mod_norm(xn_ref[...], gain, shift, scale).astype(BF16)
    row = lax.broadcasted_iota(jnp.int32, (tm + 2 * hl, 1), 0)
    inside = ((row >= hl) | (t > 0)) & ((row < hl + tm) | (t < tiles_per_seq - 1))

    base = hl - CONV_W // 2
    win = CONV_RB + 2 * hl
    cw_blk = 256
    for cblk in range(D // cw_blk):
        c0 = cblk * cw_blk
        a = _dot(h_scr[...], w1_ref[:, c0:c0 + cw_blk]) + b1_ref[:, c0:c0 + cw_blk]
        g = _dot(h_scr[...], w1_ref[:, D + c0:D + c0 + cw_blk]) + b1_ref[:, D + c0:D + c0 + cw_blk]
        ue_scr[:, c0:c0 + cw_blk] = jnp.where(inside, a * jax.nn.sigmoid(g), 0.0)
        for sub in range(cw_blk // 128):
            cols = slice(c0 + sub * 128, c0 + (sub + 1) * 128)
            for rb in range(tm // CONV_RB):
                r0 = rb * CONV_RB
                window = ue_scr[r0:r0 + win, cols]
                acc = jnp.zeros((CONV_RB, 128), F32)
                for b in range(8):
                    shifted = window if b == 0 else pltpu.roll(window, win - b, axis=0)
                    for a8 in range((base + CONV_W - 1) // 8 + 1):
                        k = 8 * a8 + b - base
                        if 0 <= k < CONV_W:
                            acc = acc + cw_ref[k:k + 1, cols] * shifted[8 * a8:8 * a8 + CONV_RB]
                y_scr[r0:r0 + CONV_RB, cols] = acc + cb_ref[:, cols]

    y = y_scr[...]
    mu = jnp.mean(y, axis=-1, keepdims=True)
    yc = y - mu
    var = jnp.mean(yc * yc, axis=-1, keepdims=True)
    z = yc * lax.rsqrt(var + EPS) * lg_ref[...] + lb_ref[...]
    z = (z * jax.nn.sigmoid(z)).astype(BF16)
    o_ref[...] = x_ref[...] + gate_ref[0] * (_dot(z, w2_ref[...]) + b2_ref[...])


def conv_mixer(x, sh, sc, gate, gain, w1, b1, cw, cb, lg, lb, w2, b2, *, tm, seq, row_of):
    m = x.shape[0]
    hb = tm // CONV_HALO
    last_hb = m // CONV_HALO - 1

    def resident(shape):
        return pl.BlockSpec(shape, lambda i: (0, 0), pipeline_mode=pl.Buffered(1))

    return pl.pallas_call(
        functools.partial(_conv_mixer_kernel, tm=tm, tiles_per_seq=seq // tm),
        grid=(m // tm,),
        in_specs=[
            pl.BlockSpec((CONV_HALO, D), lambda i: (jnp.maximum(i * hb - 1, 0), 0)),
            pl.BlockSpec((tm, D), lambda i: (i, 0)),
            pl.BlockSpec((CONV_HALO, D), lambda i: (jnp.minimum((i + 1) * hb, last_hb), 0)),
            _mod_spec(row_of), _mod_spec(row_of), _mod_spec(row_of), _vec_spec(D),
            resident((D, 2 * D)), _vec_spec(2 * D),
            resident((CONV_W, D)), _vec_spec(D), _vec_spec(D), _vec_spec(D),
            resident((D, D)), _vec_spec(D),
        ],
        out_specs=pl.BlockSpec((tm, D), lambda i: (i, 0)),
        out_shape=jax.ShapeDtypeStruct((m, D), F32),
        scratch_shapes=[
            pltpu.VMEM((tm + 2 * CONV_HALO, D), BF16),
            pltpu.VMEM((tm + 2 * CONV_HALO, D), F32),
            pltpu.VMEM((tm, D), F32),
        ],
        compiler_params=_params("parallel"),
        name="conv_mixer",
    )(x, x, x, sh, sc, gate, gain, w1, b1, cw, cb, lg, lb, w2, b2)


def _rope_tables():
    t = np.arange(SEQ)
    row = (t // GRID_W).astype(np.float32)
    col = (t % GRID_W).astype(np.float32)
    n_freq = RET_DK // 4
    inv_freq = jnp.power(ROPE_BASE, -jnp.arange(n_freq, dtype=F32) / n_freq)
    ang = jnp.concatenate([row[:, None] * inv_freq, col[:, None] * inv_freq], axis=-1)
    return jnp.cos(ang), jnp.sin(ang)


def kernel(x, c, ctx, c_ctx, ada_w, ada_b, norm_mix, norm_ffn, na_w_qkv, na_w_o, na_q_gain, na_k_gain, na_rpb, ret_w_q, ret_w_k, ret_w_v, ret_w_g, ret_w_o, ret_gn_gain, ret_decay_logit, cv_w_pw1, cv_b_pw1, cv_w_dw, cv_b_dw, cv_ln_g, cv_ln_b, cv_w_pw2, cv_b_pw2, ffn_w_up, ffn_w_dw, ffn_b_dw, ffn_w_down):
    xs = x.reshape(BATCH * SEQ, D)
    cs = ctx.reshape(BATCH * CTX, D)
    tm_x, tm_c = 512, CTX
    tiles_per_batch = SEQ // tm_x

    def x_row(i):
        return i // tiles_per_batch

    def c_row(i):
        return CTX_ROW

    c_rows = jnp.zeros((MOD_ROWS, D), F32).at[:BATCH].set(c).at[CTX_ROW].set(c_ctx)
    mods = ada_table(c_rows, ada_w, ada_b).reshape(DEPTH, MOD_ROWS, 6, 1, D)
    zero_bias = jnp.zeros((1, D), F32)
    cos, sin = _rope_tables()
    w_qkv, w_o_na = na_w_qkv.astype(BF16), na_w_o.astype(BF16)
    w_up, w_down = ffn_w_up.astype(BF16), ffn_w_down.astype(BF16)
    ffn_cb = ffn_b_dw.reshape(DEPTH, 1, FFN_H)

    for i in range(DEPTH):
        need_ctx = i < DEPTH - 1
        mod = [mods[i, :, k] for k in range(6)]
        g_mix = norm_mix[i].reshape(1, D)
        g_ffn = norm_ffn[i].reshape(1, D)
        kind, j = i % 3, i // 3
        if kind == 0:
            qg = (jnp.tile(na_q_gain[j], NA_HEADS) * (NA_SCALE * LOG2E)).reshape(1, D)
            kg = jnp.tile(na_k_gain[j], NA_HEADS).reshape(1, D)
            qkv_x = na_qkv(xs, mod[0], mod[1], g_mix, w_qkv, qg, kg, layer=j, tm=1024, row_of=lambda t: t // 4)
            qkv_c = na_qkv(cs, mod[0], mod[1], g_mix, w_qkv, qg, kg, layer=j, tm=tm_c, row_of=c_row)
            ox, oc = na_attention(qkv_x, qkv_c, na_rpb_rows(na_rpb[j]), need_ctx=need_ctx)
            xs = proj_residual(ox, w_o_na, zero_bias, mod[2], xs, layer=j, tm=1024, row_of=lambda t: t // 4)
            if need_ctx:
                cs = proj_residual(oc, w_o_na, zero_bias, mod[2], cs, layer=j, tm=tm_c, row_of=c_row)
        elif kind == 1:
            w = jnp.concatenate([ret_w_q[j], ret_w_k[j], ret_w_v[j], ret_w_g[j]], axis=1).astype(BF16)
            px = ret_proj(xs, mod[0], mod[1], g_mix, w, cos, sin, tm=tm_x, seq=SEQ, row_of=x_row, rope=True)
            pc = ret_proj(cs, mod[0], mod[1], g_mix, w, cos[:CTX], sin[:CTX], tm=tm_c, seq=CTX, row_of=c_row,
                          rope=False)
            dl = jnp.broadcast_to(ret_decay_logit[j][:, :, None, None], (2, RET_HEADS, 8, 128))
            ox, oc = ret_scan(px, pc, dl)
            w_o = ret_w_o[j].astype(BF16)
            gn = ret_gn_gain[j].reshape(1, RET_HEADS * RET_DV)
            xs = ret_out(ox, px, gn, w_o, mod[2], xs, tm=tm_x, row_of=x_row)
            if need_ctx:
                cs = ret_out(oc, pc, gn, w_o, mod[2], cs, tm=tm_c, row_of=c_row)
        else:
            conv_args = (mod[0], mod[1], mod[2], g_mix, cv_w_pw1[j].astype(BF16), cv_b_pw1[j].reshape(1, 2 * D),
                         cv_w_dw[j], cv_b_dw[j].reshape(1, D), cv_ln_g[j].reshape(1, D), cv_ln_b[j].reshape(1, D),
                         cv_w_pw2[j].astype(BF16), cv_b_pw2[j].reshape(1, D))
            xs = conv_mixer(xs, *conv_args, tm=tm_x, seq=SEQ, row_of=x_row)
            if need_ctx:
                cs = conv_mixer(cs, *conv_args, tm=tm_c, seq=CTX, row_of=c_row)

        ffn_args = (mod[3], mod[4], mod[5], g_ffn, w_up, ffn_w_dw, ffn_cb, w_down)
        xs = conv_ffn(xs, *ffn_args, layer=i, tm=256, seq=SEQ, row_of=lambda t: t // 16)
        if need_ctx:
            cs = conv_ffn(cs, *ffn_args, layer=i, tm=tm_c, seq=CTX, row_of=c_row)

    return xs.reshape(BATCH, SEQ, D)
```

```python
import functools

import numpy as np
import jax
import jax.numpy as jnp
from jax import lax
from jax.experimental import pallas as pl
from jax.experimental.pallas import tpu as pltpu

D = 1024
BATCH = 4
SEQ = 4096
DEPTH = 4
CTX = 256
GRID_W = 64
ROWS = SEQ // GRID_W
EPS = 1e-6

NA_HEADS = 16
NA_DH = 64
NA_KH = 8
NA_KW = 16
NA_SCALE = NA_DH ** -0.5
NA_PAIRS = NA_HEADS // 2
NA_WIN = NA_KH * GRID_W
NA_GROUP = 4
NA_CB = NA_KW
NA_NCB = GRID_W // NA_CB

RET_HEADS = 4
RET_DK = 256
RET_DV = 512
RET_C = 256
assert CTX == RET_C and (SEQ // RET_C) % 2 == 0
ROPE_BASE = 10000.0

CONV_W = 31
CONV_HALO = 16
CONV_RB = 128
FFN_H = 2816
FFN_SPLITS = (0, FFN_H)

MOD_ROWS = 8
CTX_ROW = BATCH
NEG = -1e30
LOG2E = 1.4426950408889634

F32 = jnp.float32
BF16 = jnp.bfloat16

VMEM_LIMIT = 52 * 1024 * 1024


def _params(*sem):
    return pltpu.CompilerParams(dimension_semantics=sem, vmem_limit_bytes=VMEM_LIMIT)


def _mod_norm(x, gain, shift, scale):
    ms = jnp.mean(x * x, axis=-1, keepdims=True)
    return x * lax.rsqrt(ms + EPS) * gain * (1.0 + scale) + shift


def _dot(a, b):
    return jnp.dot(a, b, preferred_element_type=F32)


def _dot_nt(a, b):
    return lax.dot_general(a, b, (((1,), (1,)), ((), ())), preferred_element_type=F32)


def _dot_tn(a, b):
    return lax.dot_general(a, b, (((0,), (0,)), ((), ())), preferred_element_type=F32)


def _ada_kernel(c_ref, w_ref, b_ref, o_ref):
    c = c_ref[...]
    s = (c * jax.nn.sigmoid(c)).astype(BF16)
    o_ref[0] = _dot(s, w_ref[0].astype(BF16)) + b_ref[0]


def ada_table(c_rows, ada_w, ada_b):
    tn = 1536
    return pl.pallas_call(
        _ada_kernel,
        grid=(DEPTH, 6 * D // tn),
        in_specs=[
            pl.BlockSpec((MOD_ROWS, D), lambda l, j: (0, 0)),
            pl.BlockSpec((1, D, tn), lambda l, j: (l, 0, j)),
            pl.BlockSpec((1, 1, tn), lambda l, j: (l, 0, j)),
        ],
        out_specs=pl.BlockSpec((1, MOD_ROWS, tn), lambda l, j: (l, 0, j)),
        out_shape=jax.ShapeDtypeStruct((DEPTH, MOD_ROWS, 6 * D), F32),
        compiler_params=_params("parallel", "parallel"),
        name="ada_table",
    )(c_rows, ada_w, ada_b.reshape(DEPTH, 1, 6 * D))


def _mod_spec(row_of):
    return pl.BlockSpec((1, 1, D), lambda i, *_: (row_of(i), 0, 0))


def _vec_spec(n):
    return pl.BlockSpec((1, n), lambda i, *_: (0, 0))


def _na_qkv_kernel(x_ref, sh_ref, sc_ref, gain_ref, w_ref, qg_ref, kg_ref, o_ref):
    h = _mod_norm(x_ref[...], gain_ref[...], sh_ref[0], sc_ref[0]).astype(BF16)
    gw = 256
    r = lax.broadcasted_iota(jnp.int32, (gw, gw), 0) // NA_DH
    c = lax.broadcasted_iota(jnp.int32, (gw, gw), 1) // NA_DH
    seg = jnp.where(r == c, 1.0, 0.0).astype(BF16)
    for sec, g_ref in ((0, qg_ref), (1, kg_ref)):
        y = _dot(h, w_ref[:, sec * D:(sec + 1) * D])
        for g in range(D // gw):
            yg = y[:, g * gw:(g + 1) * gw]
            ss = _dot((yg * yg).astype(BF16), seg)
            yn = yg * lax.rsqrt(ss * (1.0 / NA_DH) + EPS) * g_ref[:, g * gw:(g + 1) * gw]
            o_ref[:, sec * D + g * gw: sec * D + (g + 1) * gw] = yn.astype(o_ref.dtype)
    o_ref[:, 2 * D:] = _dot(h, w_ref[:, 2 * D:]).astype(o_ref.dtype)


def na_qkv(x, sh, sc, gain, w, qg, kg, *, layer, tm, row_of):
    m = x.shape[0]
    return pl.pallas_call(
        _na_qkv_kernel,
        grid=(m // tm,),
        in_specs=[
            pl.BlockSpec((tm, D), lambda i: (i, 0)),
            _mod_spec(row_of), _mod_spec(row_of), _vec_spec(D),
            pl.BlockSpec((None, D, 3 * D), lambda i: (layer, 0, 0)),
            _vec_spec(D), _vec_spec(D),
        ],
        out_specs=pl.BlockSpec((tm, 3 * D), lambda i: (i, 0)),
        out_shape=jax.ShapeDtypeStruct((m, 3 * D), BF16),
        compiler_params=_params("parallel"),
        name="na_qkv",
    )(x, sh, sc, gain, w, qg, kg)


def _split_heads(q, in_a):
    zero = jnp.zeros_like(q)
    return jnp.concatenate([jnp.where(in_a, q, zero), jnp.where(in_a, zero, q)], axis=0)


def _live_key_blocks(group):
    lo = min(max(8 * group - NA_KW // 2, 0), GRID_W - NA_KW)
    hi = min(max(8 * group + 7 - NA_KW // 2, 0), GRID_W - NA_KW) + NA_KW - 1
    return lo // NA_CB, hi // NA_CB + 1


def _na_attn_kernel(q_ref, k_ref, v_ref, kc_ref, vc_ref, rpb_ref, *rest, need_ctx):
    if need_ctx:
        qc_ref, o_ref, oc_ref, kb_scr, vb_scr, s_scr, p_scr, l_scr, toep_scr, bias_scr = rest
    else:
        o_ref, kb_scr, vb_scr, s_scr, p_scr, l_scr, toep_scr, bias_scr = rest
    in_a = lax.broadcasted_iota(jnp.int32, (1, 128), 1) < NA_DH
    kc = kc_ref[...]
    vc = vc_ref[...]
    blk = NA_KH * NA_CB

    def relayout(row, carry):
        dst = pl.ds(pl.multiple_of(row * NA_CB, NA_CB), NA_CB)
        for cb in range(NA_NCB):
            src = pl.ds(pl.multiple_of(row * GRID_W + cb * NA_CB, NA_CB), NA_CB)
            kb_scr[cb, dst, :] = k_ref[src, :]
            vb_scr[cb, dst, :] = v_ref[src, :]
        return carry

    lax.fori_loop(0, ROWS, relayout, 0)
    p_scr[...] = jnp.zeros_like(p_scr)

    def window(r):
        if isinstance(r, int):
            rs = min(max(r - NA_KH // 2, 0), ROWS - NA_KH)
            return r * GRID_W, rs * NA_CB, r - rs
        rs = jnp.clip(r - NA_KH // 2, 0, ROWS - NA_KH)
        return pl.multiple_of(r * GRID_W, GRID_W), pl.multiple_of(rs * NA_CB, NA_CB), r - rs

    def key_window(ref, w0):
        return jnp.concatenate([ref[cb, pl.ds(w0, blk), :] for cb in range(NA_NCB)], axis=0)

    def live(group):
        b0, b1 = _live_key_blocks(group % (GRID_W // 8))
        return slice(8 * group, 8 * group + 8), slice(b0 * blk, b1 * blk)

    n_groups8 = 2 * GRID_W // 8
    n_rel_r = 2 * NA_KH - 1

    @pl.when(pl.program_id(1) == 0)
    def _():
        c_idx = lax.broadcasted_iota(jnp.int32, (GRID_W, 128), 0)
        kc_idx = lax.broadcasted_iota(jnp.int32, (GRID_W, 128), 1)
        cs = jnp.clip(c_idx - NA_KW // 2, 0, GRID_W - NA_KW)
        valid = (kc_idx >= cs) & (kc_idx < cs + NA_KW)
        for head in range(2):
            for rr in range(n_rel_r):
                row = jnp.broadcast_to(rpb_ref[0, head, rr:rr + 1, :], (GRID_W, 128))
                toep = pltpu.roll(row, 128 - (GRID_W - 1), 1, stride=1, stride_axis=0)
                toep_scr[rr, head * GRID_W:(head + 1) * GRID_W, :] = jnp.where(valid, toep, NEG)

        def per_offset(var, carry):
            for g in range(n_groups8):
                rows, _ = live(g)
                b0, b1 = _live_key_blocks(g % (GRID_W // 8))
                for i in range(NA_KH):
                    src = toep_scr[i - var + NA_KH - 1, rows, :]
                    for cb in range(b0, b1):
                        dst = cb * blk + i * NA_CB
                        bias_scr[var, rows, dst:dst + NA_CB] = src[:, cb * NA_CB:(cb + 1) * NA_CB]
            return carry

        lax.fori_loop(0, NA_KH, per_offset, 0)

    def scores(r, slot):
        q0, w0, var = window(r)
        q2 = _split_heads(q_ref[pl.ds(q0, GRID_W), :], in_a)
        s_loc = _dot_nt(q2, key_window(kb_scr, w0))
        for g in range(n_groups8):
            rows, lanes = live(g)
            s_scr[slot, rows, lanes] = s_loc[rows, lanes] + bias_scr[var, rows, lanes]
        s_scr[slot, :, NA_WIN:] = _dot_nt(q2, kc)

    def softmax(slot):
        for g in range(n_groups8):
            rows, lanes = live(g)
            s = jnp.concatenate([s_scr[slot, rows, lanes], s_scr[slot, rows, NA_WIN:]], axis=1)
            p = jnp.exp2(s - jnp.max(s, axis=-1, keepdims=True))
            l_scr[slot, rows, :] = jnp.broadcast_to(jnp.sum(p, axis=-1, keepdims=True), (8, 128))
            n_loc = lanes.stop - lanes.start
            p_scr[slot, rows, lanes] = p[:, :n_loc]
            p_scr[slot, rows, NA_WIN:] = p[:, n_loc:]

    def values(r, slot):
        q0, w0, _ = window(r)
        p = p_scr[slot].astype(BF16)
        v = jnp.concatenate([key_window(vb_scr, w0), vc], axis=0)
        o2 = _dot(p, v) / l_scr[slot]
        o_ref[pl.ds(q0, GRID_W), :] = jnp.where(in_a, o2[:GRID_W], o2[GRID_W:]).astype(o_ref.dtype)

    def stage_values(t):
        for u in range(NA_GROUP):
            values(NA_GROUP * t + u, NA_GROUP * (t % 2) + u)

    def stage_softmax(t):
        for u in range(NA_GROUP):
            softmax(NA_GROUP * (t % 2) + u)

    def stage_scores(t):
        for u in range(NA_GROUP):
            scores(NA_GROUP * t + u, NA_GROUP * (t % 2) + u)

    n_groups = ROWS // NA_GROUP
    stage_scores(0)
    stage_softmax(0)
    stage_scores(1)

    def step(t, carry):
        stage_values(t - 2)
        stage_softmax(t - 1)
        stage_scores(t)
        return carry

    lax.fori_loop(2, n_groups, step, 0)
    stage_values(n_groups - 2)
    stage_softmax(n_groups - 1)
    stage_values(n_groups - 1)

    if need_ctx:
        q2 = _split_heads(qc_ref[...], in_a)
        s = _dot_nt(q2, kc)
        p = jnp.exp2(s - jnp.max(s, axis=-1, keepdims=True))
        o2 = _dot(p.astype(BF16), vc) / jnp.sum(p, axis=-1, keepdims=True)
        oc_ref[...] = jnp.where(in_a, o2[:CTX], o2[CTX:]).astype(oc_ref.dtype)


def na_attention(qkv_x, qkv_c, rpb, *, need_ctx):
    def col(sec):
        return lambda p, b: (b, sec * NA_PAIRS + p)
    in_specs = [
        pl.BlockSpec((SEQ, 128), col(0)),
        pl.BlockSpec((SEQ, 128), col(1)),
        pl.BlockSpec((SEQ, 128), col(2)),
        pl.BlockSpec((CTX, 128), col(1)),
        pl.BlockSpec((CTX, 128), col(2)),
        pl.BlockSpec((1, 2, 2 * NA_KH, 128), lambda p, b: (p, 0, 0, 0)),
    ]
    args = [qkv_x, qkv_x, qkv_x, qkv_c, qkv_c, rpb]
    out_specs = [pl.BlockSpec((SEQ, 128), lambda p, b: (b, p))]
    out_shape = [jax.ShapeDtypeStruct((BATCH * SEQ, D), BF16)]
    if need_ctx:
        in_specs.append(pl.BlockSpec((CTX, 128), col(0)))
        args.append(qkv_c)
        out_specs.append(pl.BlockSpec((CTX, 128), lambda p, b: (b, p)))
        out_shape.append(jax.ShapeDtypeStruct((BATCH * CTX, D), BF16))
    out = pl.pallas_call(
        functools.partial(_na_attn_kernel, need_ctx=need_ctx),
        grid=(NA_PAIRS, BATCH),
        in_specs=in_specs,
        out_specs=out_specs,
        out_shape=out_shape,
        scratch_shapes=[
            pltpu.VMEM((NA_NCB, ROWS * NA_CB, 128), BF16),
            pltpu.VMEM((NA_NCB, ROWS * NA_CB, 128), BF16),
            pltpu.VMEM((2 * NA_GROUP, 2 * GRID_W, NA_WIN + CTX), F32),
            pltpu.VMEM((2 * NA_GROUP, 2 * GRID_W, NA_WIN + CTX), F32),
            pltpu.VMEM((2 * NA_GROUP, 2 * GRID_W, 128), F32),
            pltpu.VMEM((2 * NA_KH - 1, 2 * GRID_W, 128), F32),
            pltpu.VMEM((NA_KH, 2 * GRID_W, NA_WIN), F32),
        ],
        compiler_params=_params("parallel", "arbitrary"),
        name="na_attention",
    )(*args)
    return (out[0], out[1]) if need_ctx else (out[0], None)


def na_rpb_rows(rpb):
    lpad = GRID_W - NA_KW
    p = jnp.pad(rpb * LOG2E, ((0, 0), (0, 1), (lpad, 128 - lpad - (2 * NA_KW - 1))))
    return p.reshape(NA_PAIRS, 2, 2 * NA_KH, 128)


def _proj_res_kernel(a_ref, w_ref, b_ref, gate_ref, res_ref, o_ref):
    y = _dot(a_ref[...], w_ref[...]) + b_ref[...]
    o_ref[...] = res_ref[...] + gate_ref[0] * y


def proj_residual(a, w, b, gate, res, *, layer, tm, row_of):
    m, k = a.shape
    return pl.pallas_call(
        _proj_res_kernel,
        grid=(m // tm,),
        in_specs=[
            pl.BlockSpec((tm, k), lambda i: (i, 0)),
            pl.BlockSpec((None, k, D), lambda i: (layer, 0, 0)),
            _vec_spec(D), _mod_spec(row_of),
            pl.BlockSpec((tm, D), lambda i: (i, 0)),
        ],
        out_specs=pl.BlockSpec((tm, D), lambda i: (i, 0)),
        out_shape=jax.ShapeDtypeStruct((m, D), F32),
        compiler_params=_params("parallel"),
        name="proj_residual",
    )(a, w, b, gate, res)


def _ffn_kernel(xp_ref, x_ref, xn_ref, sh_ref, sc_ref, gate_ref, gain_ref, wup_ref, cw_ref, cb_ref, wd_ref,
                o_ref, h_scr, *, tm, tiles_per_seq):
    hl = 16
    gain, shift, scale = gain_ref[...], sh_ref[0], sc_ref[0]
    t = pl.program_id(0) % tiles_per_seq
    hp = _mod_norm(xp_ref[...], gain, shift, scale)
    hn = _mod_norm(xn_ref[...], gain, shift, scale)
    h_scr[0:hl] = jnp.where(t == 0, 0.0, hp).astype(BF16)
    h_scr[hl:hl + tm] = _mod_norm(x_ref[...], gain, shift, scale).astype(BF16)
    h_scr[hl + tm:] = jnp.where(t == tiles_per_seq - 1, 0.0, hn).astype(BF16)

    acc = None
    for c0, c1 in zip(FFN_SPLITS[:-1], FFN_SPLITS[1:]):
        u = _dot(h_scr[...], wup_ref[:, c0:c1])
        v = _dot(h_scr[hl:hl + tm], wup_ref[:, FFN_H + c0:FFN_H + c1])
        uc = (cw_ref[0:1, c0:c1] * u[hl - 1:hl - 1 + tm] + cw_ref[1:2, c0:c1] * u[hl:hl + tm]
              + cw_ref[2:3, c0:c1] * u[hl + 1:hl + 1 + tm] + cb_ref[:, c0:c1])
        g = (jax.nn.gelu(uc, approximate=True) * v).astype(BF16)
        part = _dot(g, wd_ref[c0:c1, :])
        acc = part if acc is None else acc + part
    o_ref[...] = x_ref[...] + gate_ref[0] * acc


def conv_ffn(x, sh, sc, gate, gain, w_up, cw, cb, wd, *, layer, tm, seq, row_of):
    m = x.shape[0]
    hb = tm // 16
    last_hb = m // 16 - 1

    def resident(shape):
        return pl.BlockSpec((None,) + shape, lambda i: (layer, 0, 0), pipeline_mode=pl.Buffered(1))

    return pl.pallas_call(
        functools.partial(_ffn_kernel, tm=tm, tiles_per_seq=seq // tm),
        grid=(m // tm,),
        in_specs=[
            pl.BlockSpec((16, D), lambda i: (jnp.maximum(i * hb - 1, 0), 0)),
            pl.BlockSpec((tm, D), lambda i: (i, 0)),
            pl.BlockSpec((16, D), lambda i: (jnp.minimum((i + 1) * hb, last_hb), 0)),
            _mod_spec(row_of), _mod_spec(row_of), _mod_spec(row_of), _vec_spec(D),
            resident((D, 2 * FFN_H)), resident((3, FFN_H)), resident((1, FFN_H)), resident((FFN_H, D)),
        ],
        out_specs=pl.BlockSpec((tm, D), lambda i: (i, 0)),
        out_shape=jax.ShapeDtypeStruct((m, D), F32),
        scratch_shapes=[pltpu.VMEM((tm + 32, D), BF16)],
        compiler_params=_params("parallel"),
        name="conv_ffn",
    )(x, x, x, sh, sc, gate, gain, w_up, cw, cb, wd)


def _ret_proj_kernel(x_ref, sh_ref, sc_ref, gain_ref, w_ref, cos_ref, sin_ref, o_ref, *, rope):
    h = _mod_norm(x_ref[...], gain_ref[...], sh_ref[0], sc_ref[0]).astype(BF16)
    qk_width = 2 * RET_HEADS * RET_DK
    v_end = qk_width + RET_HEADS * RET_DV
    k_scale = RET_DK ** -0.5
    for hh in range(2 * RET_HEADS):
        y = _dot(h, w_ref[:, hh * RET_DK:(hh + 1) * RET_DK])
        s = k_scale if hh >= RET_HEADS else 1.0
        x1, x2 = y[:, :128], y[:, 128:]
        if rope:
            cos, sin = cos_ref[...], sin_ref[...]
            x1, x2 = x1 * cos - x2 * sin, x1 * sin + x2 * cos
        o_ref[:, hh * RET_DK: hh * RET_DK + 128] = (x1 * s).astype(o_ref.dtype)
        o_ref[:, hh * RET_DK + 128: (hh + 1) * RET_DK] = (x2 * s).astype(o_ref.dtype)
    o_ref[:, qk_width:v_end] = _dot(h, w_ref[:, qk_width:v_end]).astype(o_ref.dtype)
    g = _dot(h, w_ref[:, v_end:])
    o_ref[:, v_end:] = (g * jax.nn.sigmoid(g)).astype(o_ref.dtype)


def ret_proj(x, sh, sc, gain, w, cos, sin, *, tm, seq, row_of, rope):
    m = x.shape[0]
    n = w.shape[1]
    tps = seq // tm
    return pl.pallas_call(
        functools.partial(_ret_proj_kernel, rope=rope),
        grid=(m // tm,),
        in_specs=[
            pl.BlockSpec((tm, D), lambda i: (i, 0)),
            _mod_spec(row_of), _mod_spec(row_of), _vec_spec(D),
            pl.BlockSpec((D, n), lambda i: (0, 0), pipeline_mode=pl.Buffered(1)),
            pl.BlockSpec((tm, 128), lambda i: (i % tps, 0)),
            pl.BlockSpec((tm, 128), lambda i: (i % tps, 0)),
        ],
        out_specs=pl.BlockSpec((tm, n), lambda i: (i, 0)),
        out_shape=jax.ShapeDtypeStruct((m, n), BF16),
        compiler_params=_params("parallel"),
        name="ret_proj",
    )(x, sh, sc, gain, w, cos, sin)


def _log_sigmoid(x):
    return jnp.minimum(x, 0.0) - jnp.log1p(jnp.exp(-jnp.abs(x)))


def _ret_scan_kernel(qx_ref, kx_ref, vx_ref, qc_ref, kc_ref, vc_ref, dl_ref, ox_ref, oc_ref,
                     sf_scr, sb_scr, ax_scr, mask_scr):
    C = RET_C
    n = SEQ // C
    pos = lax.broadcasted_iota(jnp.int32, (C, 128), 0).astype(F32)
    ii = lax.broadcasted_iota(jnp.int32, (C, C), 0).astype(F32)
    jj = lax.broadcasted_iota(jnp.int32, (C, C), 1).astype(F32)

    def lanes(x, width):
        return jnp.concatenate([x] * (width // 128), axis=1)

    def decays(direction):
        lg = jnp.broadcast_to(_log_sigmoid(dl_ref[direction, 0])[0:1, :], (C, 128))
        diff = ii - jj if direction == 0 else jj - ii
        p = pos if direction == 0 else (C - 1.0) - pos
        intra = jnp.where(diff >= 0, jnp.exp(jnp.maximum(diff, 0.0) * lanes(lg, C)), 0.0)
        qd = jnp.exp((p + 1.0) * lg)
        kd = jnp.exp((C - 1.0 - p) * lg)
        cd = jnp.exp(C * lg)[0:1]
        return intra, qd, kd, cd

    intra_f, qd_f, kd_f, cd_f = decays(0)
    intra_b, qd_b, kd_b, cd_b = decays(1)
    mask_scr[...] = intra_f + intra_b

    def group_norm(o):
        mu = jnp.mean(o, axis=-1, keepdims=True)
        oc = o - mu
        return oc * lax.rsqrt(jnp.mean(oc * oc, axis=-1, keepdims=True) + EPS)

    def intra_term(q, k, v):
        return _dot((_dot_nt(q, k) * mask_scr[...]).astype(BF16), v)

    def cross_term(q, s_scr, qd):
        return lanes(qd, RET_DV) * _dot(q, s_scr[...].astype(BF16))

    def state_term(k, v, kd):
        return _dot_tn((k.astype(F32) * lanes(kd, RET_DK)).astype(BF16), v)

    def advance(s_scr, k, v, kd, cd):
        s_scr[...] = s_scr[...] * lanes(cd, RET_DV) + state_term(k, v, kd)

    q, k, v = qc_ref[...], kc_ref[...], vc_ref[...]
    oc_ref[...] = group_norm(intra_term(q, k, v)).astype(oc_ref.dtype)
    sf_scr[...] = state_term(k, v, kd_f)
    sb_scr[...] = state_term(k, v, kd_b)

    def visit(t, finish):
        rf = pl.multiple_of(t * C, C)
        rb = pl.multiple_of((n - 1 - t) * C, C)
        q, k, v = qx_ref[pl.ds(rf, C), :], kx_ref[pl.ds(rf, C), :], vx_ref[pl.ds(rf, C), :]
        o = intra_term(q, k, v) + cross_term(q, sf_scr, qd_f)
        if finish:
            ox_ref[pl.ds(rf, C), :] = group_norm(ax_scr[pl.ds(rf, C), :] + o).astype(ox_ref.dtype)
        else:
            ax_scr[pl.ds(rf, C), :] = o
        advance(sf_scr, k, v, kd_f, cd_f)
        q, k, v = qx_ref[pl.ds(rb, C), :], kx_ref[pl.ds(rb, C), :], vx_ref[pl.ds(rb, C), :]
        o = cross_term(q, sb_scr, qd_b)
        if finish:
            ox_ref[pl.ds(rb, C), :] = group_norm(ax_scr[pl.ds(rb, C), :] + o).astype(ox_ref.dtype)
        else:
            ax_scr[pl.ds(rb, C), :] = o
        advance(sb_scr, k, v, kd_b, cd_b)

    def first_visits(t, carry):
        visit(t, False)
        return carry

    def second_visits(t, carry):
        visit(t, True)
        return carry

    lax.fori_loop(0, n // 2, first_visits, 0, unroll=8)
    lax.fori_loop(n // 2, n, second_visits, 0, unroll=8)


def ret_scan(px, pc, dl_tiles):
    kb = RET_HEADS
    vb = 2 * RET_HEADS * RET_DK // RET_DV
    return pl.pallas_call(
        _ret_scan_kernel,
        grid=(BATCH, RET_HEADS),
        in_specs=[
            pl.BlockSpec((SEQ, RET_DK), lambda b, h: (b, h)),
            pl.BlockSpec((SEQ, RET_DK), lambda b, h: (b, kb + h)),
            pl.BlockSpec((SEQ, RET_DV), lambda b, h: (b, vb + h)),
            pl.BlockSpec((CTX, RET_DK), lambda b, h: (b, h)),
            pl.BlockSpec((CTX, RET_DK), lambda b, h: (b, kb + h)),
            pl.BlockSpec((CTX, RET_DV), lambda b, h: (b, vb + h)),
            pl.BlockSpec((2, 1, 8, 128), lambda b, h: (0, h, 0, 0)),
        ],
        out_specs=[
            pl.BlockSpec((SEQ, RET_DV), lambda b, h: (b, h)),
            pl.BlockSpec((CTX, RET_DV), lambda b, h: (b, h)),
        ],
        out_shape=[
            jax.ShapeDtypeStruct((BATCH * SEQ, RET_HEADS * RET_DV), BF16),
            jax.ShapeDtypeStruct((BATCH * CTX, RET_HEADS * RET_DV), BF16),
        ],
        scratch_shapes=[
            pltpu.VMEM((RET_DK, RET_DV), F32),
            pltpu.VMEM((RET_DK, RET_DV), F32),
            pltpu.VMEM((SEQ, RET_DV), F32),
            pltpu.VMEM((RET_C, RET_C), F32),
        ],
        compiler_params=_params("parallel", "parallel"),
        name="ret_scan",
    )(px, px, px, pc, pc, pc, dl_tiles)


def _ret_out_kernel(o_ref, g_ref, gn_ref, w_ref, gate_ref, res_ref, out_ref):
    a = (o_ref[...].astype(F32) * gn_ref[...] * g_ref[...].astype(F32)).astype(BF16)
    out_ref[...] = res_ref[...] + gate_ref[0] * _dot(a, w_ref[...])


def ret_out(o, proj, gn, w, gate, res, *, tm, row_of):
    m = o.shape[0]
    dv = RET_HEADS * RET_DV
    return pl.pallas_call(
        _ret_out_kernel,
        grid=(m // tm,),
        in_specs=[
            pl.BlockSpec((tm, dv), lambda i: (i, 0)),
            pl.BlockSpec((tm, dv), lambda i: (i, 2)),
            _vec_spec(dv),
            pl.BlockSpec((dv, D), lambda i: (0, 0), pipeline_mode=pl.Buffered(1)),
            _mod_spec(row_of),
            pl.BlockSpec((tm, D), lambda i: (i, 0)),
        ],
        out_specs=pl.BlockSpec((tm, D), lambda i: (i, 0)),
        out_shape=jax.ShapeDtypeStruct((m, D), F32),
        compiler_params=_params("parallel"),
        name="ret_out",
    )(o, proj, gn, w, gate, res)


def _conv_mixer_kernel(xp_ref, x_ref, xn_ref, sh_ref, sc_ref, gate_ref, gain_ref, w1_ref, b1_ref, cw_ref, cb_ref,
                       lg_ref, lb_ref, w2_ref, b2_ref, o_ref, h_scr, ue_scr, y_scr, *, tm, tiles_per_seq):
    hl = CONV_HALO
    gain, shift, scale = gain_ref[...], sh_ref[0], sc_ref[0]
    t = pl.program_id(0) % tiles_per_seq
    h_scr[0:hl] = _mod_norm(xp_ref[...], gain, shift, scale).astype(BF16)
    h_scr[hl:hl + tm] = _mod_norm(x_ref[...], gain, shift, scale).astype(BF16)
    h_scr[hl + tm:] = _mod_norm(xn_ref[...], gain, shift, scale).astype(BF16)
    row = lax.broadcasted_iota(jnp.int32, (tm + 2 * hl, 1), 0)
    inside = ((row >= hl) | (t > 0)) & ((row < hl + tm) | (t < tiles_per_seq - 1))

    base = hl - CONV_W // 2
    win = CONV_RB + 2 * hl
    cw_blk = 256
    for cblk in range(D // cw_blk):
        c0 = cblk * cw_blk
        a = _dot(h_scr[...], w1_ref[:, c0:c0 + cw_blk]) + b1_ref[:, c0:c0 + cw_blk]
        g = _dot(h_scr[...], w1_ref[:, D + c0:D + c0 + cw_blk]) + b1_ref[:, D + c0:D + c0 + cw_blk]
        ue_scr[:, c0:c0 + cw_blk] = jnp.where(inside, a * jax.nn.sigmoid(g), 0.0)
        for sub in range(cw_blk // 128):
            cols = slice(c0 + sub * 128, c0 + (sub + 1) * 128)
            for rb in range(tm // CONV_RB):
                r0 = rb * CONV_RB
                window = ue_scr[r0:r0 + win, cols]
                acc = jnp.zeros((CONV_RB, 128), F32)
                for b in range(8):
                    shifted = window if b == 0 else pltpu.roll(window, win - b, axis=0)
                    for a8 in range((base + CONV_W - 1) // 8 + 1):
                        k = 8 * a8 + b - base
                        if 0 <= k < CONV_W:
                            acc = acc + cw_ref[k:k + 1, cols] * shifted[8 * a8:8 * a8 + CONV_RB]
                y_scr[r0:r0 + CONV_RB, cols] = acc + cb_ref[:, cols]

    y = y_scr[...]
    mu = jnp.mean(y, axis=-1, keepdims=True)
    yc = y - mu
    var = jnp.mean(yc * yc, axis=-1, keepdims=True)
    z = yc * lax.rsqrt(var + EPS) * lg_ref[...] + lb_ref[...]
    z = (z * jax.nn.sigmoid(z)).astype(BF16)
    o_ref[...] = x_ref[...] + gate_ref[0] * (_dot(z, w2_ref[...]) + b2_ref[...])


def conv_mixer(x, sh, sc, gate, gain, w1, b1, cw, cb, lg, lb, w2, b2, *, tm, seq, row_of):
    m = x.shape[0]
    hb = tm // CONV_HALO
    last_hb = m // CONV_HALO - 1

    def resident(shape):
        return pl.BlockSpec(shape, lambda i: (0, 0), pipeline_mode=pl.Buffered(1))

    return pl.pallas_call(
        functools.partial(_conv_mixer_kernel, tm=tm, tiles_per_seq=seq // tm),
        grid=(m // tm,),
        in_specs=[
            pl.BlockSpec((CONV_HALO, D), lambda i: (jnp.maximum(i * hb - 1, 0), 0)),
            pl.BlockSpec((tm, D), lambda i: (i, 0)),
            pl.BlockSpec((CONV_HALO, D), lambda i: (jnp.minimum((i + 1) * hb, last_hb), 0)),
            _mod_spec(row_of), _mod_spec(row_of), _mod_spec(row_of), _vec_spec(D),
            resident((D, 2 * D)), _vec_spec(2 * D),
            resident((CONV_W, D)), _vec_spec(D), _vec_spec(D), _vec_spec(D),
            resident((D, D)), _vec_spec(D),
        ],
        out_specs=pl.BlockSpec((tm, D), lambda i: (i, 0)),
        out_shape=jax.ShapeDtypeStruct((m, D), F32),
        scratch_shapes=[
            pltpu.VMEM((tm + 2 * CONV_HALO, D), BF16),
            pltpu.VMEM((tm + 2 * CONV_HALO, D), F32),
            pltpu.VMEM((tm, D), F32),
        ],
        compiler_params=_params("parallel"),
        name="conv_mixer",
    )(x, x, x, sh, sc, gate, gain, w1, b1, cw, cb, lg, lb, w2, b2)


def _rope_tables():
    t = np.arange(SEQ)
    row = (t // GRID_W).astype(np.float32)
    col = (t % GRID_W).astype(np.float32)
    n_freq = RET_DK // 4
    inv_freq = jnp.power(ROPE_BASE, -jnp.arange(n_freq, dtype=F32) / n_freq)
    ang = jnp.concatenate([row[:, None] * inv_freq, col[:, None] * inv_freq], axis=-1)
    return jnp.cos(ang), jnp.sin(ang)


def kernel(x, c, ctx, c_ctx, ada_w, ada_b, norm_mix, norm_ffn, na_w_qkv, na_w_o, na_q_gain, na_k_gain, na_rpb, ret_w_q, ret_w_k, ret_w_v, ret_w_g, ret_w_o, ret_gn_gain, ret_decay_logit, cv_w_pw1, cv_b_pw1, cv_w_dw, cv_b_dw, cv_ln_g, cv_ln_b, cv_w_pw2, cv_b_pw2, ffn_w_up, ffn_w_dw, ffn_b_dw, ffn_w_down):
    xs = x.reshape(BATCH * SEQ, D)
    cs = ctx.reshape(BATCH * CTX, D)
    tm_x, tm_c = 512, CTX
    tiles_per_batch = SEQ // tm_x

    def x_row(i):
        return i // tiles_per_batch

    def c_row(i):
        return CTX_ROW

    c_rows = jnp.zeros((MOD_ROWS, D), F32).at[:BATCH].set(c).at[CTX_ROW].set(c_ctx)
    mods = ada_table(c_rows, ada_w, ada_b).reshape(DEPTH, MOD_ROWS, 6, 1, D)
    zero_bias = jnp.zeros((1, D), F32)
    cos, sin = _rope_tables()
    w_qkv, w_o_na = na_w_qkv.astype(BF16), na_w_o.astype(BF16)
    w_up, w_down = ffn_w_up.astype(BF16), ffn_w_down.astype(BF16)
    ffn_cb = ffn_b_dw.reshape(DEPTH, 1, FFN_H)

    for i in range(DEPTH):
        need_ctx = i < DEPTH - 1
        mod = [mods[i, :, k] for k in range(6)]
        g_mix = norm_mix[i].reshape(1, D)
        g_ffn = norm_ffn[i].reshape(1, D)
        kind, j = i % 3, i // 3
        if kind == 0:
            qg = (jnp.tile(na_q_gain[j], NA_HEADS) * (NA_SCALE * LOG2E)).reshape(1, D)
            kg = jnp.tile(na_k_gain[j], NA_HEADS).reshape(1, D)
            qkv_x = na_qkv(xs, mod[0], mod[1], g_mix, w_qkv, qg, kg, layer=j, tm=1024, row_of=lambda t: t // 4)
            qkv_c = na_qkv(cs, mod[0], mod[1], g_mix, w_qkv, qg, kg, layer=j, tm=tm_c, row_of=c_row)
            ox, oc = na_attention(qkv_x, qkv_c, na_rpb_rows(na_rpb[j]), need_ctx=need_ctx)
            xs = proj_residual(ox, w_o_na, zero_bias, mod[2], xs, layer=j, tm=1024, row_of=lambda t: t // 4)
            if need_ctx:
                cs = proj_residual(oc, w_o_na, zero_bias, mod[2], cs, layer=j, tm=tm_c, row_of=c_row)
        elif kind == 1:
            w = jnp.concatenate([ret_w_q[j], ret_w_k[j], ret_w_v[j], ret_w_g[j]], axis=1).astype(BF16)
            px = ret_proj(xs, mod[0], mod[1], g_mix, w, cos, sin, tm=1024, seq=SEQ, row_of=lambda t: t // 4, rope=True)
            pc = ret_proj(cs, mod[0], mod[1], g_mix, w, cos[:CTX], sin[:CTX], tm=tm_c, seq=CTX, row_of=c_row,
                          rope=False)
            dl = jnp.broadcast_to(ret_decay_logit[j][:, :, None, None], (2, RET_HEADS, 8, 128))
            ox, oc = ret_scan(px, pc, dl)
            w_o = ret_w_o[j].astype(BF16)
            gn = ret_gn_gain[j].reshape(1, RET_HEADS * RET_DV)
            xs = ret_out(ox, px, gn, w_o, mod[2], xs, tm=1024, row_of=lambda t: t // 4)
            if need_ctx:
                cs = ret_out(oc, pc, gn, w_o, mod[2], cs, tm=tm_c, row_of=c_row)
        else:
            conv_args = (mod[0], mod[1], mod[2], g_mix, cv_w_pw1[j].astype(BF16), cv_b_pw1[j].reshape(1, 2 * D),
                         cv_w_dw[j], cv_b_dw[j].reshape(1, D), cv_ln_g[j].reshape(1, D), cv_ln_b[j].reshape(1, D),
                         cv_w_pw2[j].astype(BF16), cv_b_pw2[j].reshape(1, D))
            xs = conv_mixer(xs, *conv_args, tm=tm_x, seq=SEQ, row_of=x_row)
            if need_ctx:
                cs = conv_mixer(cs, *conv_args, tm=tm_c, seq=CTX, row_of=c_row)

        ffn_args = (mod[3], mod[4], mod[5], g_ffn, w_up, ffn_w_dw, ffn_cb, w_down)
        xs = conv_ffn(xs, *ffn_args, layer=i, tm=256, seq=SEQ, row_of=lambda t: t // 16)
        if need_ctx:
            cs = conv_ffn(cs, *ffn_args, layer=i, tm=tm_c, seq=CTX, row_of=c_row)

    return xs.reshape(BATCH, SEQ, D)
```
